```python
import jax, jax.numpy as jnp
from jax import lax
import numpy as np

D_MODEL = 1024
BATCH = 8
SEQ = 2048
DEPTH = 2

HEAD_DIM = 64
D_MIX = D_MODEL
RG_WIDTH = 3 * D_MIX // 8
HG_WIDTH = 3 * D_MIX // 8
SC_WIDTH = D_MIX - RG_WIDTH - HG_WIDTH
RG_HEADS = RG_WIDTH // HEAD_DIM
HG_HEADS = HG_WIDTH // HEAD_DIM
RG_CONV = 4
SC_CONV = 3
FFN_CONV = 3
RG_C = 8.0
HG_CHUNK = 64
D_FF = 2816
PLE_DIM = 256
EPS = 1e-6
PROJ_SIZES = [RG_WIDTH, RG_WIDTH,
              HG_WIDTH, HG_WIDTH, HG_WIDTH, HG_WIDTH,
              SC_WIDTH, SC_WIDTH, SC_WIDTH]
N_PROJ = sum(PROJ_SIZES)
PROJ_SPLITS = [int(s) for s in np.cumsum(PROJ_SIZES)[:-1]]

kernel_name = 'hymba_style_rglru_hgrn2_shortconv_hybrid'


def rmsnorm(x, gain):
    xf = x.astype(jnp.float32)
    y = xf * lax.rsqrt(jnp.mean(xf * xf, axis=-1, keepdims=True) + EPS)
    return (y * gain.astype(jnp.float32)).astype(x.dtype)


def head_rmsnorm(y, gain):
    b, t, w = y.shape
    yh = y.reshape(b, t, w // HEAD_DIM, HEAD_DIM)
    yh = yh * lax.rsqrt(jnp.mean(yh * yh, axis=-1, keepdims=True) + EPS)
    return yh.reshape(b, t, w) * gain.astype(jnp.float32)


def causal_dwconv(x, w):
    k = w.shape[0]
    return lax.conv_general_dilated(
        x, w[:, None, :].astype(x.dtype), window_strides=(1,), padding=[(k - 1, 0)],
        dimension_numbers=('NWC', 'WIO', 'NWC'), feature_group_count=x.shape[-1])


def _linear_combine(c1, c2):
    a1, b1 = c1
    a2, b2 = c2
    return a1 * a2, a2 * b1 + b2


def rglru_group(u_in, gate_in, conv_w, conv_b, w_r, b_r, w_i, b_i, lam, gain):
    b, t, _ = u_in.shape
    u = (causal_dwconv(u_in, conv_w) + conv_b.astype(u_in.dtype)).astype(jnp.float32)
    uh = u.reshape(b, t, RG_HEADS, HEAD_DIM)
    r = jax.nn.sigmoid(jnp.einsum('btnd,nde->btne', uh, w_r.astype(jnp.float32))
                       + b_r.astype(jnp.float32)).reshape(b, t, RG_WIDTH)
    ig = jax.nn.sigmoid(jnp.einsum('btnd,nde->btne', uh, w_i.astype(jnp.float32))
                        + b_i.astype(jnp.float32)).reshape(b, t, RG_WIDTH)
    log_a = -RG_C * r * jax.nn.softplus(-lam.astype(jnp.float32))
    a = jnp.exp(log_a)
    beta = jnp.sqrt(-jnp.expm1(2.0 * log_a))
    _, h = lax.associative_scan(_linear_combine, (a, beta * ig * u), axis=1)
    return head_rmsnorm(h * jax.nn.gelu(gate_in.astype(jnp.float32)), gain)


def hgrn2_group(q, f_raw, v, g, lb, gain):
    b, t, _ = q.shape
    nc = t // HG_CHUNK
    qf = jax.nn.silu(q.astype(jnp.float32)) * HEAD_DIM ** -0.5
    f = lb + (1.0 - lb) * jax.nn.sigmoid(f_raw.astype(jnp.float32))
    log_f = jnp.log(f)
    k = 1.0 - f

    def to_chunks(u):
        return u.reshape(b, nc, HG_CHUNK, HG_HEADS, HEAD_DIM).transpose(1, 0, 3, 2, 4)

    mask = jnp.tril(jnp.ones((HG_CHUNK, HG_CHUNK), dtype=bool))[:, :, None]

    def step(S, inp):
        qc, kc, vc, lfc = inp
        G = jnp.cumsum(lfc, axis=2)
        o_inter = jnp.einsum('bhtk,bhkv->bhtv', qc * jnp.exp(G), S)
        diff = G[:, :, :, None, :] - G[:, :, None, :, :]
        decay = jnp.where(mask, jnp.exp(jnp.minimum(diff, 0.0)), 0.0)
        scores = jnp.einsum('bhtk,bhsk,bhtsk->bhts', qc, kc, decay)
        o_intra = jnp.einsum('bhts,bhsv->bhtv', scores, vc)
        G_last = G[:, :, -1:, :]
        S_new = (jnp.exp(G_last[:, :, 0, :])[..., None] * S
                 + jnp.einsum('bhsk,bhsv->bhkv', kc * jnp.exp(G_last - G), vc))
        return S_new, o_inter + o_intra

    S0 = jnp.zeros((b, HG_HEADS, HEAD_DIM, HEAD_DIM), jnp.float32)
    _, o = lax.scan(step, S0, (to_chunks(qf), to_chunks(k),
                               to_chunks(v.astype(jnp.float32)), to_chunks(log_f)))
    o = o.transpose(1, 0, 3, 2, 4).reshape(b, t, HG_WIDTH)
    return head_rmsnorm(o, gain) * jax.nn.silu(g.astype(jnp.float32))


def shortconv_group(bg, cg, v, conv_w, gain):
    y = bg * causal_dwconv(cg * v, conv_w)
    return head_rmsnorm(y.astype(jnp.float32), gain)


def conv_ffn(h, w_up, conv_w, w_down):
    gu = causal_dwconv(h @ w_up, conv_w)
    g, u = jnp.split(gu, 2, axis=-1)
    return (jax.nn.silu(g) * u) @ w_down


def setup_inputs(seed: int = 0) -> dict:
    key = jax.random.key(seed)
    ks = jax.random.split(key, 24)
    n = jax.random.normal
    f32 = jnp.float32
    u = jax.random.uniform(ks[10], (DEPTH, RG_WIDTH), f32, 0.9, 0.999)
    a0 = u ** (1.0 / RG_C)
    rg_lambda = jnp.log(a0) - jnp.log1p(-a0)
    return {
        'x': n(ks[0], (BATCH, SEQ, D_MODEL), f32),
        'p': n(ks[1], (DEPTH, BATCH, SEQ, PLE_DIM), f32),
        'norm_mix_gain': 1.0 + 0.02 * n(ks[2], (DEPTH, D_MODEL), f32),
        'w_in': n(ks[3], (DEPTH, D_MODEL, N_PROJ), f32) * D_MODEL ** -0.5,
        'rg_conv_w': n(ks[4], (DEPTH, RG_CONV, RG_WIDTH), f32) * RG_CONV ** -0.5,
        'rg_conv_b': 0.01 * n(ks[5], (DEPTH, RG_WIDTH), f32),
        'rg_w_r': n(ks[6], (DEPTH, RG_HEADS, HEAD_DIM, HEAD_DIM), f32) * HEAD_DIM ** -0.5,
        'rg_b_r': 0.01 * n(ks[7], (DEPTH, RG_HEADS, HEAD_DIM), f32),
        'rg_w_i': n(ks[8], (DEPTH, RG_HEADS, HEAD_DIM, HEAD_DIM), f32) * HEAD_DIM ** -0.5,
        'rg_b_i': 0.01 * n(ks[9], (DEPTH, RG_HEADS, HEAD_DIM), f32),
        'rg_lambda': rg_lambda,
        'hg_lower_bounds': 0.1 * n(ks[11], (DEPTH, HG_WIDTH), f32),
        'sc_conv_w': n(ks[12], (DEPTH, SC_CONV, SC_WIDTH), f32) * SC_CONV ** -0.5,
        'mix_out_gain': 1.0 + 0.02 * n(ks[13], (DEPTH, D_MIX), f32),
        'w_out': n(ks[14], (DEPTH, D_MIX, D_MODEL), f32) * D_MIX ** -0.5,
        'norm_ffn_gain': 1.0 + 0.02 * n(ks[15], (DEPTH, D_MODEL), f32),
        'ffn_w_up': n(ks[16], (DEPTH, D_MODEL, 2 * D_FF), f32) * D_MODEL ** -0.5,
        'ffn_conv_w': n(ks[17], (DEPTH, FFN_CONV, 2 * D_FF), f32) * FFN_CONV ** -0.5,
        'ffn_w_down': n(ks[18], (DEPTH, D_FF, D_MODEL), f32) * D_FF ** -0.5,
        'ple_w_proj': n(ks[19], (DEPTH, PLE_DIM, D_MODEL), f32) * PLE_DIM ** -0.5,
        'ple_w_gate': n(ks[20], (DEPTH, D_MODEL, D_MODEL), f32) * D_MODEL ** -0.5,
        'final_norm_gain': 1.0 + 0.02 * n(ks[21], (D_MODEL,), f32),
    }


def reference(x, p, norm_mix_gain, w_in, rg_conv_w, rg_conv_b, rg_w_r, rg_b_r, rg_w_i, rg_b_i,
              rg_lambda, hg_lower_bounds, sc_conv_w, mix_out_gain, w_out, norm_ffn_gain,
              ffn_w_up, ffn_conv_w, ffn_w_down, ple_w_proj, ple_w_gate, final_norm_gain):
    dt = x.dtype
    sm = jax.nn.softmax(hg_lower_bounds.astype(jnp.float32), axis=0)
    lower_bound = jnp.cumsum(sm, axis=0) - sm[0:1]
    for i in range(DEPTH):
        h = rmsnorm(x, norm_mix_gain[i])
        z = h @ w_in[i]
        a_u, a_gate, b_q, b_f, b_v, b_g, c_b, c_c, c_v = jnp.split(z, PROJ_SPLITS, axis=-1)
        g_a, g_b, g_c = jnp.split(mix_out_gain[i], [RG_WIDTH, RG_WIDTH + HG_WIDTH])
        y_a = rglru_group(a_u, a_gate, rg_conv_w[i], rg_conv_b[i], rg_w_r[i], rg_b_r[i],
                          rg_w_i[i], rg_b_i[i], rg_lambda[i], g_a)
        y_b = hgrn2_group(b_q, b_f, b_v, b_g, lower_bound[i], g_b)
        y_c = shortconv_group(c_b, c_c, c_v, sc_conv_w[i], g_c)
        y = jnp.concatenate([y_a, y_b, y_c], axis=-1).astype(dt)
        x = x + y @ w_out[i]
        x = x + conv_ffn(rmsnorm(x, norm_ffn_gain[i]), ffn_w_up[i], ffn_conv_w[i], ffn_w_down[i])
        gate = jax.nn.sigmoid((x @ ple_w_gate[i]).astype(jnp.float32))
        x = x + (gate * (p[i] @ ple_w_proj[i]).astype(jnp.float32)).astype(dt)
    return rmsnorm(x, final_norm_gain)
```

```python
import functools

import jax
import jax.numpy as jnp
from jax import lax
from jax.experimental import pallas as pl
from jax.experimental.pallas import tpu as pltpu

HEAD_DIM = 64
RG_WIDTH = 384
HG_WIDTH = 384
SC_WIDTH = 256
RG_CONV = 4
SC_CONV = 3
FFN_CONV = 3
RG_C = 8.0
HG_CHUNK = 64
EPS = 1e-6
HALO = 8
LANES = 128
TIME_BLOCK = 256
VMEM_LIMIT_BYTES = 56 * 1024 * 1024

F32 = jnp.float32
BF16 = jnp.bfloat16


def _dot(a, b):
    return jnp.dot(a.astype(BF16), b.astype(BF16), preferred_element_type=F32)


def _dot_nt(a, b):
    return lax.dot_general(a.astype(BF16), b.astype(BF16), (((1,), (1,)), ((), ())),
                           preferred_element_type=F32)


def _dot_tn(a, b):
    return lax.dot_general(a.astype(BF16), b.astype(BF16), (((0,), (0,)), ((), ())),
                           preferred_element_type=F32)


def _rmsnorm(x, gain):
    ms = jnp.mean(x * x, axis=-1, keepdims=True)
    return x * lax.rsqrt(ms + EPS) * gain


def _head_rmsnorm(y, ones_bd):
    sq = y * y
    hi = sq.astype(BF16)
    lo = (sq - hi.astype(F32)).astype(BF16)
    ssum = (jnp.dot(hi, ones_bd, preferred_element_type=F32)
            + jnp.dot(lo, ones_bd, preferred_element_type=F32))
    return y * lax.rsqrt(ssum * (1.0 / HEAD_DIM) + EPS)


def _causal_conv(ext_ref, new_rows, w_ref, taps, rows):
    ext_ref[HALO:HALO + rows, :] = new_rows
    out = None
    for k in range(taps):
        term = w_ref[k:k + 1, :] * ext_ref[pl.ds(HALO - (taps - 1) + k, rows), :]
        out = term if out is None else out + term
    ext_ref[0:HALO, :] = ext_ref[rows:rows + HALO, :]
    return out


def _linear_scan(a, b, h0):
    rows = a.shape[0]
    row = lax.broadcasted_iota(jnp.int32, a.shape, 0)
    s = 1
    while s < rows:
        keep = row >= s
        a_prev = pltpu.roll(a, s, axis=0)
        b_prev = pltpu.roll(b, s, axis=0)
        b = jnp.where(keep, a * b_prev + b, b)
        a = jnp.where(keep, a * a_prev, a)
        s *= 2
    return a * h0 + b


def _chunk_cumsum(x, chunk):
    pos = lax.broadcasted_iota(jnp.int32, x.shape, 0) % chunk
    s = 1
    while s < chunk:
        x = jnp.where(pos >= s, x + pltpu.roll(x, s, axis=0), x)
        s *= 2
    return x


def _split_reference(g, half):
    n = g.shape[0]
    if half >= 4:
        pieces = []
        for blk in range(n // (2 * half)):
            r = blk * 2 * half + half - 1
            pieces.append(jnp.broadcast_to(g[r:r + 1, :], (2 * half, g.shape[1])))
        return jnp.concatenate(pieces, axis=0) if len(pieces) > 1 else pieces[0]
    pos = lax.broadcasted_iota(jnp.int32, g.shape, 0) % (2 * half)
    out = g
    for p in range(2 * half):
        d = p - (half - 1)
        if d == 0:
            continue
        shifted = pltpu.roll(g, d % n, axis=0)
        out = jnp.where(pos == p, shifted, out)
    return out


def _mixer_kernel(layer, x_ref, ng_ref, win_ref, rgw_ref, rgb_ref, wri_ref, bri_ref, lam_ref,
                  lbraw_ref, scw_ref, og_ref, ones_ref, wout_ref, o_ref,
                  ext_a, ext_c, h_carry, st_ref, q_s, k_s, v_s, g_s, o_s):
    rows = x_ref.shape[0]

    @pl.when(pl.program_id(1) == 0)
    def _():
        ext_a[0:HALO, :] = jnp.zeros((HALO, RG_WIDTH), F32)
        ext_c[0:HALO, :] = jnp.zeros((HALO, SC_WIDTH), F32)
        h_carry[...] = jnp.zeros(h_carry.shape, F32)
        st_ref[...] = jnp.zeros(st_ref.shape, F32)

    x = x_ref[...]
    z = _dot(_rmsnorm(x, ng_ref[...]), win_ref[...])
    o0 = 0
    a_u, a_gate = z[:, 0:384], z[:, 384:768]
    b_q, b_f, b_v, b_g = z[:, 768:1152], z[:, 1152:1536], z[:, 1536:1920], z[:, 1920:2304]
    c_b, c_c, c_v = z[:, 2304:2560], z[:, 2560:2816], z[:, 2816:3072]
    ones_bd = ones_ref[...]
    gains = og_ref[...]

    u = _causal_conv(ext_a, a_u, rgw_ref, RG_CONV, rows) + rgb_ref[...]
    ri = _dot(u, wri_ref[...]) + bri_ref[...]
    r = jax.nn.sigmoid(ri[:, :RG_WIDTH])
    ig = jax.nn.sigmoid(ri[:, RG_WIDTH:])
    log_a = -RG_C * r * jax.nn.softplus(-lam_ref[...])
    a = jnp.exp(log_a)
    th = jnp.tanh(log_a)
    beta = jnp.sqrt(-2.0 * th / (1.0 - th))
    h = _linear_scan(a, beta * ig * u, h_carry[0:1, :])
    h_carry[...] = jnp.broadcast_to(h[rows - 1:rows, :], h_carry.shape)
    y_a = _head_rmsnorm(h * jax.nn.gelu(a_gate), ones_bd) * gains[:, 0:RG_WIDTH]

    raw = lbraw_ref[...]
    e = jnp.exp(raw - jnp.max(raw, axis=0, keepdims=True))
    sm = e / jnp.sum(e, axis=0, keepdims=True)
    lb = sm[0:1, :] - sm[0:1, :]
    for j in range(1, layer + 1):
        lb = lb + sm[j:j + 1, :]
    f = lb + (1.0 - lb) * jax.nn.sigmoid(b_f)
    q_s[...] = jax.nn.silu(b_q) * (HEAD_DIM ** -0.5)
    k_s[...] = 1.0 - f
    v_s[...] = b_v
    g_s[...] = _chunk_cumsum(jnp.log(f), HG_CHUNK)

    ti = lax.broadcasted_iota(jnp.int32, (HG_CHUNK, HG_CHUNK), 0)
    si = lax.broadcasted_iota(jnp.int32, (HG_CHUNK, HG_CHUNK), 1)
    top = ti ^ si
    halves = (32, 16, 8, 4, 2, 1)
    level_masks = [(top >= hf) & (top < 2 * hf) & (ti > si) for hf in halves]
    diag_mask = ti == si
    lane = lax.broadcasted_iota(jnp.int32, (HG_CHUNK, LANES), 1)
    head_sel = [lane < HEAD_DIM, lane >= HEAD_DIM]
    sr = lax.broadcasted_iota(jnp.int32, (LANES, LANES), 0) // HEAD_DIM
    sc = lax.broadcasted_iota(jnp.int32, (LANES, LANES), 1) // HEAD_DIM
    same_head = sr == sc

    def chunk_body(c, carry):
        r0 = pl.multiple_of(c * HG_CHUNK, HG_CHUNK)
        for p in range(HG_WIDTH // LANES):
            ls = slice(p * LANES, (p + 1) * LANES)
            gc = g_s[pl.ds(r0, HG_CHUNK), ls]
            qc = q_s[pl.ds(r0, HG_CHUNK), ls]
            kc = k_s[pl.ds(r0, HG_CHUNK), ls]
            vc = v_s[pl.ds(r0, HG_CHUNK), ls]
            g_last = gc[HG_CHUNK - 1:HG_CHUNK, :]
            st = st_ref[p]
            o_inter = _dot_nt(qc * jnp.exp(gc), st)
            kd = kc * jnp.exp(g_last - gc)
            st_ref[p] = st * jnp.exp(g_last) + jnp.where(same_head, _dot_tn(vc, kd), 0.0)
            kc_b = kc.astype(BF16)
            scores = [jnp.where(diag_mask, _dot_nt(jnp.where(hs, qc, 0.0), kc_b), 0.0) for hs in head_sel]
            for hf, mask in zip(halves, level_masks):
                w = jnp.exp(-jnp.abs(gc - _split_reference(gc, hf)))
                qt = qc * w
                kt = (kc * w).astype(BF16)
                for j, hs in enumerate(head_sel):
                    scores[j] = jnp.where(mask, _dot_nt(jnp.where(hs, qt, 0.0), kt), scores[j])
            o_intra = jnp.where(head_sel[0], _dot(scores[0], vc), _dot(scores[1], vc))
            o_s[pl.ds(r0, HG_CHUNK), ls] = o_inter + o_intra
        return carry

    lax.fori_loop(0, rows // HG_CHUNK, chunk_body, 0)
    y_b = (_head_rmsnorm(o_s[...], ones_bd) * gains[:, RG_WIDTH:RG_WIDTH + HG_WIDTH]
           * jax.nn.silu(b_g))

    conv = _causal_conv(ext_c, c_c * c_v, scw_ref, SC_CONV, rows)
    y_c = (_head_rmsnorm(c_b * conv, ones_bd[:SC_WIDTH, :SC_WIDTH])
           * gains[:, RG_WIDTH + HG_WIDTH:])

    y = jnp.concatenate([y_a, y_b, y_c], axis=-1)
    o_ref[...] = x + _dot(y, wout_ref[...])


def _ffn_kernel(last, x_ref, p_ref, ng_ref, wup_ref, cw_ref, wdown_ref, wgate_ref, wproj_ref,
                fg_ref, o_ref, ext):
    rows = x_ref.shape[0]
    d_ff = wdown_ref.shape[0]

    @pl.when(pl.program_id(1) == 0)
    def _():
        ext[0:HALO, :] = jnp.zeros((HALO, ext.shape[1]), F32)

    x = x_ref[...]
    gu = _causal_conv(ext, _dot(_rmsnorm(x, ng_ref[...]), wup_ref[...]), cw_ref, FFN_CONV, rows)
    act = jax.nn.silu(gu[:, :d_ff]) * gu[:, d_ff:]
    x = x + _dot(act, wdown_ref[...])
    gate = jax.nn.sigmoid(_dot(x, wgate_ref[...]))
    x = x + gate * _dot(p_ref[...], wproj_ref[...])
    if last:
        x = _rmsnorm(x, fg_ref[...])
    o_ref[...] = x


def _const_spec(shape):
    zeros = (0,) * len(shape)
    return pl.BlockSpec(shape, lambda b, t: zeros, pipeline_mode=pl.Buffered(1))


def _block_diag(w):
    n, d, e = w.shape
    eye = jnp.eye(n, dtype=w.dtype)
    return jnp.einsum('nde,nm->ndme', w, eye).reshape(n * d, n * e)


def _mixer_call(layer, x, ng, win, rgw, rgb, wri, bri, lam, lbraw, scw, og, ones_bd, wout):
    batch, seq, d_model = x.shape
    tb = min(TIME_BLOCK, seq)
    assert seq % tb == 0 and tb % HG_CHUNK == 0
    row_spec = pl.BlockSpec((None, tb, d_model), lambda b, t: (b, t, 0))
    consts = (ng, win, rgw, rgb, wri, bri, lam, lbraw, scw, og, ones_bd, wout)
    return pl.pallas_call(
        functools.partial(_mixer_kernel, layer),
        grid=(batch, seq // tb),
        in_specs=[row_spec] + [_const_spec(c.shape) for c in consts],
        out_specs=row_spec,
        out_shape=jax.ShapeDtypeStruct(x.shape, x.dtype),
        scratch_shapes=[
            pltpu.VMEM((tb + HALO, RG_WIDTH), F32),
            pltpu.VMEM((tb + HALO, SC_WIDTH), F32),
            pltpu.VMEM((HALO, RG_WIDTH), F32),
            pltpu.VMEM((HG_WIDTH // LANES, LANES, LANES), F32),
            pltpu.VMEM((tb, HG_WIDTH), F32),
            pltpu.VMEM((tb, HG_WIDTH), F32),
            pltpu.VMEM((tb, HG_WIDTH), F32),
            pltpu.VMEM((tb, HG_WIDTH), F32),
            pltpu.VMEM((tb, HG_WIDTH), F32),
        ],
        compiler_params=pltpu.CompilerParams(
            dimension_semantics=("arbitrary", "arbitrary"), vmem_limit_bytes=VMEM_LIMIT_BYTES),
        name=f"mixer_l{layer}",
    )(x, *consts)


def _ffn_call(layer, last, x, p, ng, wup, cw, wdown, wgate, wproj, fg):
    batch, seq, d_model = x.shape
    tb = min(TIME_BLOCK, seq)
    assert seq % tb == 0
    row_spec = pl.BlockSpec((None, tb, d_model), lambda b, t: (b, t, 0))
    p_spec = pl.BlockSpec((None, tb, p.shape[-1]), lambda b, t: (b, t, 0))
    consts = (ng, wup, cw, wdown, wgate, wproj, fg)
    return pl.pallas_call(
        functools.partial(_ffn_kernel, last),
        grid=(batch, seq // tb),
        in_specs=[row_spec, p_spec] + [_const_spec(c.shape) for c in consts],
        out_specs=row_spec,
        out_shape=jax.ShapeDtypeStruct(x.shape, x.dtype),
        scratch_shapes=[pltpu.VMEM((tb + HALO, wup.shape[1]), F32)],
        compiler_params=pltpu.CompilerParams(
            dimension_semantics=("arbitrary", "arbitrary"), vmem_limit_bytes=VMEM_LIMIT_BYTES),
        name=f"ffn_l{layer}",
    )(x, p, *consts)


def kernel(x, p, norm_mix_gain, w_in, rg_conv_w, rg_conv_b, rg_w_r, rg_b_r, rg_w_i, rg_b_i, rg_lambda, hg_lower_bounds, sc_conv_w, mix_out_gain, w_out, norm_ffn_gain, ffn_w_up, ffn_conv_w, ffn_w_down, ple_w_proj, ple_w_gate, final_norm_gain):
    depth = w_in.shape[0]
    head_id = jnp.arange(RG_WIDTH) // HEAD_DIM
    ones_bd = (head_id[:, None] == head_id[None, :]).astype(BF16)
    row = lambda v: v.reshape(1, -1)
    for i in range(depth):
        wri = jnp.concatenate([_block_diag(rg_w_r[i]), _block_diag(rg_w_i[i])], axis=1).astype(BF16)
        bri = jnp.concatenate([rg_b_r[i].reshape(-1), rg_b_i[i].reshape(-1)]).reshape(1, -1)
        x = _mixer_call(i, x, row(norm_mix_gain[i]), w_in[i].astype(BF16), rg_conv_w[i],
                        row(rg_conv_b[i]), wri, bri, row(rg_lambda[i]), hg_lower_bounds,
                        sc_conv_w[i], row(mix_out_gain[i]), ones_bd, w_out[i].astype(BF16))
        x = _ffn_call(i, i == depth - 1, x, p[i], row(norm_ffn_gain[i]), ffn_w_up[i].astype(BF16),
                      ffn_conv_w[i], ffn_w_down[i].astype(BF16), ple_w_gate[i].astype(BF16),
                      ple_w_proj[i].astype(BF16), row(final_norm_gain))
    return x
```

```python
import functools

import jax
import jax.numpy as jnp
from jax import lax
from jax.experimental import pallas as pl
from jax.experimental.pallas import tpu as pltpu

HEAD_DIM = 64
RG_WIDTH = 384
HG_WIDTH = 384
SC_WIDTH = 256
RG_CONV = 4
SC_CONV = 3
FFN_CONV = 3
RG_C = 8.0
HG_CHUNK = 64
EPS = 1e-6
SUBLANES = 8
LANES = 128
HALO = SUBLANES
TIME_BLOCK = 256
PROJ_TILE = 256
VMEM_LIMIT_BYTES = 56 * 1024 * 1024
LEVEL_HALVES = (32, 16, 8, 4, 2, 1)

_sizes = (RG_WIDTH, RG_WIDTH, HG_WIDTH, HG_WIDTH, HG_WIDTH, HG_WIDTH, SC_WIDTH, SC_WIDTH, SC_WIDTH)
PROJ_OFFSETS = tuple(sum(_sizes[:i]) for i in range(len(_sizes) + 1))

F32 = jnp.float32
BF16 = jnp.bfloat16


def _dot(a, b):
    return jnp.dot(a.astype(BF16), b.astype(BF16), preferred_element_type=F32)


def _dot_nt(a, b):
    return lax.dot_general(a.astype(BF16), b.astype(BF16), (((1,), (1,)), ((), ())),
                           preferred_element_type=F32)


def _dot_tn(a, b):
    return lax.dot_general(a.astype(BF16), b.astype(BF16), (((0,), (0,)), ((), ())),
                           preferred_element_type=F32)


def _rmsnorm(x, gain):
    ms = jnp.mean(x * x, axis=-1, keepdims=True)
    return x * lax.rsqrt(ms + EPS) * gain


def _head_rmsnorm(y, ones_bd):
    ssum = _dot(y * y, ones_bd)
    return y * lax.rsqrt(ssum * (1.0 / HEAD_DIM) + EPS)


def _causal_conv(ext_ref, new_rows, w_ref, taps, rows):
    ext_ref[HALO:HALO + rows, :] = new_rows
    out = None
    for k in range(taps):
        term = w_ref[k:k + 1, :] * ext_ref[pl.ds(HALO - (taps - 1) + k, rows), :]
        out = term if out is None else out + term
    ext_ref[0:HALO, :] = ext_ref[rows:rows + HALO, :]
    return out


def _tile_roll(x, shift):
    rows, w = x.shape
    x3 = x.reshape(rows // SUBLANES, SUBLANES, w)
    return pltpu.roll(x3, shift, axis=1).reshape(rows, w)


def _tile_pos(shape):
    return lax.broadcasted_iota(jnp.int32, shape, 0) % SUBLANES


def _linear_scan(a, b, h0):
    rows = a.shape[0]
    pos = _tile_pos(a.shape)
    s = 1
    while s < SUBLANES:
        keep = pos >= s
        b = jnp.where(keep, a * _tile_roll(b, s) + b, b)
        a = jnp.where(keep, a * _tile_roll(a, s), a)
        s *= 2
    tiles = []
    for t in range(rows // SUBLANES):
        sl = slice(t * SUBLANES, (t + 1) * SUBLANES)
        h = a[sl] * h0 + b[sl]
        h0 = h[SUBLANES - 1:SUBLANES, :]
        tiles.append(h)
    return jnp.concatenate(tiles, axis=0)


def _chunk_cumsum(x, chunk):
    rows = x.shape[0]
    pos = _tile_pos(x.shape)
    s = 1
    while s < SUBLANES:
        x = jnp.where(pos >= s, x + _tile_roll(x, s), x)
        s *= 2
    tiles = []
    last = None
    for t in range(rows // SUBLANES):
        blk = x[t * SUBLANES:(t + 1) * SUBLANES]
        if t % (chunk // SUBLANES) != 0:
            blk = blk + last
        last = blk[SUBLANES - 1:SUBLANES, :]
        tiles.append(blk)
    return jnp.concatenate(tiles, axis=0)


def _block_row_broadcast(g, block, row):
    pieces = [jnp.broadcast_to(g[b * block + row:b * block + row + 1, :], (block, g.shape[1]))
              for b in range(g.shape[0] // block)]
    return jnp.concatenate(pieces, axis=0)


def _level_weight(g, f, half):
    if half >= 4:
        return jnp.exp(-jnp.abs(g - _block_row_broadcast(g, 2 * half, half - 1)))
    pos = lax.broadcasted_iota(jnp.int32, f.shape, 0) % (2 * half)
    if half == 1:
        return jnp.where(pos == 1, f, 1.0)
    nxt = _tile_roll(f, SUBLANES - 1)
    prv = _tile_roll(f, 1)
    return jnp.where(pos == 0, nxt, jnp.where(pos == 1, 1.0, jnp.where(pos == 2, f, f * prv)))


def _mixer_kernel(layer, steps_per_seq, x_ref, xn_ref, ng_ref, win_ref, rgw_ref, rgb_ref, wri_ref,
                  bri_ref, lam_ref, lbraw_ref, scw_ref, og_ref, ones_ref, wout_ref, o_ref,
                  z_even, z_odd, ext_a, ext_c, h_carry, st_ref, o_s):
    step = pl.program_id(0)

    def projection_tiles(src_ref, dst_ref):
        hn = _rmsnorm(src_ref[...], ng_ref[...]).astype(BF16)

        def tile(j):
            cols = slice(j * PROJ_TILE, (j + 1) * PROJ_TILE)
            dst_ref[:, cols] = jnp.dot(hn, win_ref[:, cols], preferred_element_type=F32)

        return [functools.partial(tile, j) for j in range(win_ref.shape[1] // PROJ_TILE)]

    @pl.when(step == 0)
    def _():
        for tile in projection_tiles(x_ref, z_even):
            tile()

    @pl.when(step % steps_per_seq == 0)
    def _():
        ext_a[0:HALO, :] = jnp.zeros((HALO, RG_WIDTH), F32)
        ext_c[0:HALO, :] = jnp.zeros((HALO, SC_WIDTH), F32)
        h_carry[...] = jnp.zeros(h_carry.shape, F32)
        st_ref[...] = jnp.zeros(st_ref.shape, F32)

    mix = functools.partial(
        _mixer_step, layer, x_ref, rgw_ref=rgw_ref, rgb_ref=rgb_ref, wri_ref=wri_ref,
        bri_ref=bri_ref, lam_ref=lam_ref, lbraw_ref=lbraw_ref, scw_ref=scw_ref, og_ref=og_ref,
        ones_ref=ones_ref, wout_ref=wout_ref, o_ref=o_ref, ext_a=ext_a, ext_c=ext_c,
        h_carry=h_carry, st_ref=st_ref, o_s=o_s)

    @pl.when(step % 2 == 0)
    def _():
        mix(z_even, projection_tiles(xn_ref, z_odd))

    @pl.when(step % 2 == 1)
    def _():
        mix(z_odd, projection_tiles(xn_ref, z_even))


def _mixer_step(layer, x_ref, z_ref, side_work, *, rgw_ref, rgb_ref, wri_ref, bri_ref, lam_ref,
                lbraw_ref, scw_ref, og_ref, ones_ref, wout_ref, o_ref, ext_a, ext_c, h_carry,
                st_ref, o_s):
    side_work = list(side_work)

    def interleave(n=1):
        for _ in range(n):
            if side_work:
                side_work.pop(0)()

    rows = x_ref.shape[0]
    x = x_ref[...]
    a_u, a_gate, b_q, b_f, b_v, b_g, c_b, c_c, c_v = (
        z_ref[:, PROJ_OFFSETS[i]:PROJ_OFFSETS[i + 1]] for i in range(9))
    ones_bd = ones_ref[...]
    gains = og_ref[...]

    u = _causal_conv(ext_a, a_u, rgw_ref, RG_CONV, rows) + rgb_ref[...]
    ri = _dot(u, wri_ref[...]) + bri_ref[...]
    interleave()
    r = jax.nn.sigmoid(ri[:, :RG_WIDTH])
    ig = jax.nn.sigmoid(ri[:, RG_WIDTH:])
    log_a = -RG_C * r * jax.nn.softplus(-lam_ref[...])
    a = jnp.exp(log_a)
    th = jnp.tanh(log_a)
    beta = jnp.sqrt(-2.0 * th / (1.0 - th))
    h = _linear_scan(a, beta * ig * u, h_carry[0:1, :])
    h_carry[...] = jnp.broadcast_to(h[rows - 1:rows, :], h_carry.shape)
    interleave()
    y_a = _head_rmsnorm(h * jax.nn.gelu(a_gate), ones_bd) * gains[:, 0:RG_WIDTH]
    interleave()

    raw = lbraw_ref[...]
    e = jnp.exp(raw - jnp.max(raw, axis=0, keepdims=True))
    sm = e / jnp.sum(e, axis=0, keepdims=True)
    lb = sm[0:1, :] - sm[0:1, :]
    for j in range(1, layer + 1):
        lb = lb + sm[j:j + 1, :]
    f = lb + (1.0 - lb) * jax.nn.sigmoid(b_f)
    qh = jax.nn.silu(b_q) * (HEAD_DIM ** -0.5)
    kk = 1.0 - f
    g = _chunk_cumsum(jnp.log(f), HG_CHUNK)
    g_last = _block_row_broadcast(g, HG_CHUNK, HG_CHUNK - 1)
    q_inter = (qh * jnp.exp(g)).astype(BF16)
    k_state = (kk * jnp.exp(g_last - g)).astype(BF16)
    v_b = b_v.astype(BF16)
    level_q = [qh.astype(BF16)]
    level_k = [kk.astype(BF16)]
    for hf in LEVEL_HALVES:
        w = _level_weight(g, f, hf)
        level_q.append((qh * w).astype(BF16))
        level_k.append((kk * w).astype(BF16))

    ti = lax.broadcasted_iota(jnp.int32, (HG_CHUNK, LANES), 0)
    lane = lax.broadcasted_iota(jnp.int32, (HG_CHUNK, LANES), 1)
    si = lane % HEAD_DIM
    top = ti ^ si
    level_masks = [ti == si] + [(top >= hf) & (top < 2 * hf) & (ti > si) for hf in LEVEL_HALVES]
    head0 = lane < HEAD_DIM
    sr = lax.broadcasted_iota(jnp.int32, (LANES, LANES), 0) // HEAD_DIM
    sc = lax.broadcasted_iota(jnp.int32, (LANES, LANES), 1) // HEAD_DIM
    same_head = sr == sc

    def head_stack(m):
        zero = jnp.zeros_like(m)
        return jnp.concatenate([jnp.where(head0, m, zero), jnp.where(head0, zero, m)], axis=0)

    pairs = HG_WIDTH // LANES
    states = [st_ref[p] for p in range(pairs)]
    for c in range(rows // HG_CHUNK):
        rs = slice(c * HG_CHUNK, (c + 1) * HG_CHUNK)
        for p in range(pairs):
            ls = slice(p * LANES, (p + 1) * LANES)
            scores = None
            for mask, lq, lk in zip(level_masks, level_q, level_k):
                part = _dot_nt(lq[rs, ls], head_stack(lk[rs, ls]))
                scores = jnp.where(mask, part, 0.0 if scores is None else scores)
            o_intra = _dot(scores, head_stack(v_b[rs, ls]))
            o_inter = _dot_nt(q_inter[rs, ls], states[p])
            o_s[rs, ls] = o_inter + o_intra
            states[p] = (states[p] * jnp.exp(g_last[c * HG_CHUNK:c * HG_CHUNK + 1, ls])
                         + jnp.where(same_head, _dot_tn(v_b[rs, ls], k_state[rs, ls]), 0.0))
            interleave()
    for p in range(pairs):
        st_ref[p] = states[p]
    y_b = (_head_rmsnorm(o_s[...], ones_bd) * gains[:, RG_WIDTH:RG_WIDTH + HG_WIDTH]
           * jax.nn.silu(b_g))

    conv = _causal_conv(ext_c, c_c * c_v, scw_ref, SC_CONV, rows)
    y_c = (_head_rmsnorm(c_b * conv, ones_bd[:SC_WIDTH, :SC_WIDTH])
           * gains[:, RG_WIDTH + HG_WIDTH:])

    interleave(len(side_work))
    y = jnp.concatenate([y_a, y_b, y_c], axis=-1)
    o_ref[...] = x + _dot(y, wout_ref[...])


def _ffn_kernel(last, x_ref, p_ref, ng_ref, wup_ref, cw_ref, wdown_ref, wgate_ref, wproj_ref,
                fg_ref, o_ref, ext):
    rows = x_ref.shape[0]
    d_ff = wdown_ref.shape[0]

    @pl.when(pl.program_id(1) == 0)
    def _():
        ext[0:HALO, :] = jnp.zeros((HALO, ext.shape[1]), F32)

    x = x_ref[...]
    gu = _causal_conv(ext, _dot(_rmsnorm(x, ng_ref[...]), wup_ref[...]), cw_ref, FFN_CONV, rows)
    act = jax.nn.silu(gu[:, :d_ff]) * gu[:, d_ff:]
    x = x + _dot(act, wdown_ref[...])
    gate = jax.nn.sigmoid(_dot(x, wgate_ref[...]))
    x = x + gate * _dot(p_ref[...], wproj_ref[...])
    if last:
        x = _rmsnorm(x, fg_ref[...])
    o_ref[...] = x


def _const_spec(shape, grid_rank=2):
    zeros = (0,) * len(shape)
    index_map = (lambda s: zeros) if grid_rank == 1 else (lambda b, t: zeros)
    return pl.BlockSpec(shape, index_map, pipeline_mode=pl.Buffered(1))


def _block_diag(w):
    n, d, e = w.shape
    eye = jnp.eye(n, dtype=w.dtype)
    return jnp.einsum('nde,nm->ndme', w, eye).reshape(n * d, n * e)


def _mixer_call(layer, x, ng, win, rgw, rgb, wri, bri, lam, lbraw, scw, og, ones_bd, wout):
    batch, seq, d_model = x.shape
    tb = min(TIME_BLOCK, seq)
    assert seq % tb == 0 and tb % HG_CHUNK == 0
    nt = seq // tb
    steps = batch * nt
    row_spec = pl.BlockSpec((None, tb, d_model), lambda s: (s // nt, s % nt, 0))
    next_spec = pl.BlockSpec(
        (None, tb, d_model),
        lambda s: (jnp.minimum(s + 1, steps - 1) // nt, jnp.minimum(s + 1, steps - 1) % nt, 0))
    consts = (ng, win, rgw, rgb, wri, bri, lam, lbraw, scw, og, ones_bd, wout)
    return pl.pallas_call(
        functools.partial(_mixer_kernel, layer, nt),
        grid=(steps,),
        in_specs=[row_spec, next_spec] + [_const_spec(c.shape, 1) for c in consts],
        out_specs=row_spec,
        out_shape=jax.ShapeDtypeStruct(x.shape, x.dtype),
        scratch_shapes=[
            pltpu.VMEM((tb, win.shape[1]), F32),
            pltpu.VMEM((tb, win.shape[1]), F32),
            pltpu.VMEM((tb + HALO, RG_WIDTH), F32),
            pltpu.VMEM((tb + HALO, SC_WIDTH), F32),
            pltpu.VMEM((HALO, RG_WIDTH), F32),
            pltpu.VMEM((HG_WIDTH // LANES, LANES, LANES), F32),
            pltpu.VMEM((tb, HG_WIDTH), F32),
        ],
        compiler_params=pltpu.CompilerParams(
            dimension_semantics=("arbitrary",), vmem_limit_bytes=VMEM_LIMIT_BYTES),
        name=f"mixer_l{layer}",
    )(x, x, *consts)


def _ffn_call(layer, last, x, p, ng, wup, cw, wdown, wgate, wproj, fg):
    batch, seq, d_model = x.shape
    tb = min(TIME_BLOCK, seq)
    assert seq % tb == 0
    row_spec = pl.BlockSpec((None, tb, d_model), lambda b, t: (b, t, 0))
    p_spec = pl.BlockSpec((None, tb, p.shape[-1]), lambda b, t: (b, t, 0))
    consts = (ng, wup, cw, wdown, wgate, wproj, fg)
    return pl.pallas_call(
        functools.partial(_ffn_kernel, last),
        grid=(batch, seq // tb),
        in_specs=[row_spec, p_spec] + [_const_spec(c.shape) for c in consts],
        out_specs=row_spec,
        out_shape=jax.ShapeDtypeStruct(x.shape, x.dtype),
        scratch_shapes=[pltpu.VMEM((tb + HALO, wup.shape[1]), F32)],
        compiler_params=pltpu.CompilerParams(
            dimension_semantics=("arbitrary", "arbitrary"), vmem_limit_bytes=VMEM_LIMIT_BYTES),
        name=f"ffn_l{layer}",
    )(x, p, *consts)


def kernel(x, p, norm_mix_gain, w_in, rg_conv_w, rg_conv_b, rg_w_r, rg_b_r, rg_w_i, rg_b_i, rg_lambda, hg_lower_bounds, sc_conv_w, mix_out_gain, w_out, norm_ffn_gain, ffn_w_up, ffn_conv_w, ffn_w_down, ple_w_proj, ple_w_gate, final_norm_gain):
    depth = w_in.shape[0]
    head_id = jnp.arange(RG_WIDTH) // HEAD_DIM
    ones_bd = (head_id[:, None] == head_id[None, :]).astype(BF16)
    row = lambda v: v.reshape(1, -1)
    for i in range(depth):
        wri = jnp.concatenate([_block_diag(rg_w_r[i]), _block_diag(rg_w_i[i])], axis=1).astype(BF16)
        bri = jnp.concatenate([rg_b_r[i].reshape(-1), rg_b_i[i].reshape(-1)]).reshape(1, -1)
        x = _mixer_call(i, x, row(norm_mix_gain[i]), w_in[i].astype(BF16), rg_conv_w[i],
                        row(rg_conv_b[i]), wri, bri, row(rg_lambda[i]), hg_lower_bounds,
                        sc_conv_w[i], row(mix_out_gain[i]), ones_bd, w_out[i].astype(BF16))
        x = _ffn_call(i, i == depth - 1, x, p[i], row(norm_ffn_gain[i]), ffn_w_up[i].astype(BF16),
                      ffn_conv_w[i], ffn_w_down[i].astype(BF16), ple_w_gate[i].astype(BF16),
                      ple_w_proj[i].astype(BF16), row(final_norm_gain))
    return x
```

```python
import functools

import jax
import jax.numpy as jnp
from jax import lax
from jax.experimental import pallas as pl
from jax.experimental.pallas import tpu as pltpu

HEAD_DIM = 64
RG_WIDTH = 384
HG_WIDTH = 384
SC_WIDTH = 256
RG_CONV = 4
SC_CONV = 3
FFN_CONV = 3
RG_C = 8.0
HG_CHUNK = 64
EPS = 1e-6
SUBLANES = 8
LANES = 128
HALO = SUBLANES
TIME_BLOCK = 256
PROJ_TILE = 256
WEIGHT_CAST_ROWS = 128
VMEM_LIMIT_BYTES = 56 * 1024 * 1024
MATMUL_LEVEL_HALVES = (32, 16, 8, 4, 2)

_sizes = (RG_WIDTH, RG_WIDTH, HG_WIDTH, HG_WIDTH, HG_WIDTH, HG_WIDTH, SC_WIDTH, SC_WIDTH, SC_WIDTH)
PROJ_OFFSETS = tuple(sum(_sizes[:i]) for i in range(len(_sizes) + 1))

F32 = jnp.float32
BF16 = jnp.bfloat16


def _dot(a, b):
    return jnp.dot(a.astype(BF16), b.astype(BF16), preferred_element_type=F32)


def _dot_nt(a, b):
    return lax.dot_general(a.astype(BF16), b.astype(BF16), (((1,), (1,)), ((), ())),
                           preferred_element_type=F32)


def _dot_tn(a, b):
    return lax.dot_general(a.astype(BF16), b.astype(BF16), (((0,), (0,)), ((), ())),
                           preferred_element_type=F32)


def _rmsnorm(x, gain):
    ms = jnp.mean(x * x, axis=-1, keepdims=True)
    return x * lax.rsqrt(ms + EPS) * gain


def _head_rmsnorm(y, ones_bd):
    ssum = _dot(y * y, ones_bd)
    return y * lax.rsqrt(ssum * (1.0 / HEAD_DIM) + EPS)


def _causal_conv(ext_ref, new_rows, w_ref, taps, rows):
    ext_ref[HALO:HALO + rows, :] = new_rows
    out = None
    for k in range(taps):
        term = w_ref[k:k + 1, :] * ext_ref[pl.ds(HALO - (taps - 1) + k, rows), :]
        out = term if out is None else out + term
    ext_ref[0:HALO, :] = ext_ref[rows:rows + HALO, :]
    return out


def _tile_roll(x, shift):
    rows, w = x.shape
    x3 = x.reshape(rows // SUBLANES, SUBLANES, w)
    return pltpu.roll(x3, shift, axis=1).reshape(rows, w)


def _tile_pos(shape):
    return lax.broadcasted_iota(jnp.int32, shape, 0) % SUBLANES


def _linear_scan(a, b, h0):
    rows = a.shape[0]
    pos = _tile_pos(a.shape)
    s = 1
    while s < SUBLANES:
        keep = pos >= s
        b = jnp.where(keep, a * _tile_roll(b, s) + b, b)
        a = jnp.where(keep, a * _tile_roll(a, s), a)
        s *= 2
    tiles = []
    for t in range(rows // SUBLANES):
        sl = slice(t * SUBLANES, (t + 1) * SUBLANES)
        h = a[sl] * h0 + b[sl]
        h0 = h[SUBLANES - 1:SUBLANES, :]
        tiles.append(h)
    return jnp.concatenate(tiles, axis=0)


def _chunk_cumsum(x, chunk):
    rows = x.shape[0]
    pos = _tile_pos(x.shape)
    s = 1
    while s < SUBLANES:
        x = jnp.where(pos >= s, x + _tile_roll(x, s), x)
        s *= 2
    tiles = []
    last = None
    for t in range(rows // SUBLANES):
        blk = x[t * SUBLANES:(t + 1) * SUBLANES]
        if t % (chunk // SUBLANES) != 0:
            blk = blk + last
        last = blk[SUBLANES - 1:SUBLANES, :]
        tiles.append(blk)
    return jnp.concatenate(tiles, axis=0)


def _block_row_broadcast(g, block, row):
    pieces = [jnp.broadcast_to(g[b * block + row:b * block + row + 1, :], (block, g.shape[1]))
              for b in range(g.shape[0] // block)]
    return jnp.concatenate(pieces, axis=0)


def _level_weight(g, f, half):
    if half >= 4:
        return jnp.exp(-jnp.abs(g - _block_row_broadcast(g, 2 * half, half - 1)))
    assert half == 2
    pos = lax.broadcasted_iota(jnp.int32, f.shape, 0) % (2 * half)
    nxt = _tile_roll(f, SUBLANES - 1)
    prv = _tile_roll(f, 1)
    return jnp.where(pos == 0, nxt, jnp.where(pos == 1, 1.0, jnp.where(pos == 2, f, f * prv)))


def _mixer_kernel(layer, steps_per_seq, x_ref, xn_ref, ng_ref, win_f32, rgw_ref, rgb_ref, wri_f32,
                  bri_ref, lam_ref, lbraw_ref, scw_ref, og_ref, ones_ref, wout_f32, o_ref,
                  win_ref, wri_ref, wout_ref, z_even, z_odd, ext_a, ext_c, h_carry, st_ref, o_s):
    step = pl.program_id(0)

    @pl.when(step == 0)
    def _():
        for src, dst in ((win_f32, win_ref), (wri_f32, wri_ref), (wout_f32, wout_ref)):
            for r0 in range(0, src.shape[0], WEIGHT_CAST_ROWS):
                rs = slice(r0, min(r0 + WEIGHT_CAST_ROWS, src.shape[0]))
                dst[rs, :] = src[rs, :].astype(BF16)

    def projection_tiles(src_ref, dst_ref):
        hn = _rmsnorm(src_ref[...], ng_ref[...]).astype(BF16)

        def tile(j):
            cols = slice(j * PROJ_TILE, (j + 1) * PROJ_TILE)
            dst_ref[:, cols] = jnp.dot(hn, win_ref[:, cols], preferred_element_type=F32)

        return [functools.partial(tile, j) for j in range(win_ref.shape[1] // PROJ_TILE)]

    @pl.when(step == 0)
    def _():
        for tile in projection_tiles(x_ref, z_even):
            tile()

    @pl.when(step % steps_per_seq == 0)
    def _():
        ext_a[0:HALO, :] = jnp.zeros((HALO, RG_WIDTH), F32)
        ext_c[0:HALO, :] = jnp.zeros((HALO, SC_WIDTH), F32)
        h_carry[...] = jnp.zeros(h_carry.shape, F32)
        st_ref[...] = jnp.zeros(st_ref.shape, F32)

    mix = functools.partial(
        _mixer_step, layer, x_ref, rgw_ref=rgw_ref, rgb_ref=rgb_ref, wri_ref=wri_ref,
        bri_ref=bri_ref, lam_ref=lam_ref, lbraw_ref=lbraw_ref, scw_ref=scw_ref, og_ref=og_ref,
        ones_ref=ones_ref, wout_ref=wout_ref, o_ref=o_ref, ext_a=ext_a, ext_c=ext_c,
        h_carry=h_carry, st_ref=st_ref, o_s=o_s)

    @pl.when(step % 2 == 0)
    def _():
        mix(z_even, projection_tiles(xn_ref, z_odd))

    @pl.when(step % 2 == 1)
    def _():
        mix(z_odd, projection_tiles(xn_ref, z_even))


def _mixer_step(layer, x_ref, z_ref, side_work, *, rgw_ref, rgb_ref, wri_ref, bri_ref, lam_ref,
                lbraw_ref, scw_ref, og_ref, ones_ref, wout_ref, o_ref, ext_a, ext_c, h_carry,
                st_ref, o_s):
    side_work = list(side_work)

    def interleave(n=1):
        for _ in range(n):
            if side_work:
                side_work.pop(0)()

    rows = x_ref.shape[0]
    x = x_ref[...]
    a_u, a_gate, b_q, b_f, b_v, b_g, c_b, c_c, c_v = (
        z_ref[:, PROJ_OFFSETS[i]:PROJ_OFFSETS[i + 1]] for i in range(9))
    ones_bd = ones_ref[...]
    gains = og_ref[...]

    u = _causal_conv(ext_a, a_u, rgw_ref, RG_CONV, rows) + rgb_ref[...]
    ri = _dot(u, wri_ref[...]) + bri_ref[...]
    interleave(2)
    r = jax.nn.sigmoid(ri[:, :RG_WIDTH])
    ig = jax.nn.sigmoid(ri[:, RG_WIDTH:])
    log_a = -RG_C * r * jax.nn.softplus(-lam_ref[...])
    a = jnp.exp(log_a)
    th = jnp.tanh(log_a)
    beta = jnp.sqrt(-2.0 * th / (1.0 - th))
    h = _linear_scan(a, beta * ig * u, h_carry[0:1, :])
    h_carry[...] = jnp.broadcast_to(h[rows - 1:rows, :], h_carry.shape)
    interleave(2)
    y_a = _head_rmsnorm(h * jax.nn.gelu(a_gate), ones_bd) * gains[:, 0:RG_WIDTH]
    interleave(1)

    raw = lbraw_ref[...]
    e = jnp.exp(raw - jnp.max(raw, axis=0, keepdims=True))
    sm = e / jnp.sum(e, axis=0, keepdims=True)
    lb = sm[0:1, :] - sm[0:1, :]
    for j in range(1, layer + 1):
        lb = lb + sm[j:j + 1, :]
    f = lb + (1.0 - lb) * jax.nn.sigmoid(b_f)
    qh = jax.nn.silu(b_q) * (HEAD_DIM ** -0.5)
    kk = 1.0 - f
    g = _chunk_cumsum(jnp.log(f), HG_CHUNK)
    g_last = _block_row_broadcast(g, HG_CHUNK, HG_CHUNK - 1)
    q_inter = (qh * jnp.exp(g)).astype(BF16)
    k_state = (kk * jnp.exp(g_last - g)).astype(BF16)
    v_b = b_v.astype(BF16)
    band_e = [(qh * kk).astype(BF16), (qh * f * _tile_roll(kk, 1)).astype(BF16)]
    level_q, level_k = [], []
    for hf in MATMUL_LEVEL_HALVES:
        w = _level_weight(g, f, hf)
        level_q.append((qh * w).astype(BF16))
        level_k.append((kk * w).astype(BF16))
    interleave(1)

    ti = lax.broadcasted_iota(jnp.int32, (2 * HG_CHUNK, HEAD_DIM), 0) % HG_CHUNK
    si = lax.broadcasted_iota(jnp.int32, (2 * HG_CHUNK, HEAD_DIM), 1)
    top = ti ^ si
    band_masks = [ti == si, (top == 1) & (ti > si)]
    level_masks = [(top >= hf) & (top < 2 * hf) & (ti > si) for hf in MATMUL_LEVEL_HALVES]
    lane = lax.broadcasted_iota(jnp.int32, (HG_CHUNK, LANES), 1)
    head0 = lane < HEAD_DIM
    sr = lax.broadcasted_iota(jnp.int32, (LANES, LANES), 0) // HEAD_DIM
    sc = lax.broadcasted_iota(jnp.int32, (LANES, LANES), 1) // HEAD_DIM
    same_head = sr == sc
    head_ones = jnp.ones((LANES, HEAD_DIM), BF16)

    def head_stack(m):
        zero = jnp.zeros_like(m)
        return jnp.concatenate([jnp.where(head0, m, zero), jnp.where(head0, zero, m)], axis=0)

    pairs = HG_WIDTH // LANES
    states = [st_ref[p] for p in range(pairs)]
    def chunk_scores(c, p):
        rs = slice(c * HG_CHUNK, (c + 1) * HG_CHUNK)
        ls = slice(p * LANES, (p + 1) * LANES)
        scores = jnp.zeros((2 * HG_CHUNK, HEAD_DIM), F32)
        for mask, be in zip(band_masks, band_e):
            scores = jnp.where(mask, _dot(head_stack(be[rs, ls]), head_ones), scores)
        for mask, lq, lk in zip(level_masks, level_q, level_k):
            scores = jnp.where(mask, _dot_nt(head_stack(lq[rs, ls]), lk[rs, ls]), scores)
        return scores.astype(BF16)

    def chunk_output(c, p, scores):
        rs = slice(c * HG_CHUNK, (c + 1) * HG_CHUNK)
        ls = slice(p * LANES, (p + 1) * LANES)
        o_inter = _dot_nt(q_inter[rs, ls], states[p])
        states[p] = (states[p] * jnp.exp(g_last[c * HG_CHUNK:c * HG_CHUNK + 1, ls])
                     + jnp.where(same_head, _dot_tn(v_b[rs, ls], k_state[rs, ls]), 0.0))
        o_stack = _dot(scores, v_b[rs, ls])
        o_s[rs, ls] = o_inter + jnp.where(head0, o_stack[:HG_CHUNK], o_stack[HG_CHUNK:])

    pending = None
    for c in range(rows // HG_CHUNK):
        for p in range(pairs):
            scores = chunk_scores(c, p)
            if pending is not None:
                chunk_output(*pending)
            pending = (c, p, scores)
            interleave()
    chunk_output(*pending)
    for p in range(pairs):
        st_ref[p] = states[p]
    y_b = (_head_rmsnorm(o_s[...], ones_bd) * gains[:, RG_WIDTH:RG_WIDTH + HG_WIDTH]
           * jax.nn.silu(b_g))

    conv = _causal_conv(ext_c, c_c * c_v, scw_ref, SC_CONV, rows)
    y_c = (_head_rmsnorm(c_b * conv, ones_bd[:SC_WIDTH, :SC_WIDTH])
           * gains[:, RG_WIDTH + HG_WIDTH:])

    interleave(len(side_work))
    y = jnp.concatenate([y_a, y_b, y_c], axis=-1)
    o_ref[...] = x + _dot(y, wout_ref[...])


def _ffn_kernel(last, x_ref, p_ref, ng_ref, wup_ref, cw_ref, wdown_ref, wgate_ref, wproj_ref,
                fg_ref, o_ref, ext):
    rows = x_ref.shape[0]
    d_ff = wdown_ref.shape[0]

    @pl.when(pl.program_id(1) == 0)
    def _():
        ext[0:HALO, :] = jnp.zeros((HALO, ext.shape[1]), F32)

    x = x_ref[...]
    gu = _causal_conv(ext, _dot(_rmsnorm(x, ng_ref[...]), wup_ref[...]), cw_ref, FFN_CONV, rows)
    act = jax.nn.silu(gu[:, :d_ff]) * gu[:, d_ff:]
    x = x + _dot(act, wdown_ref[...])
    gate = jax.nn.sigmoid(_dot(x, wgate_ref[...]))
    x = x + gate * _dot(p_ref[...], wproj_ref[...])
    if last:
        x = _rmsnorm(x, fg_ref[...])
    o_ref[...] = x


def _const_spec(shape, grid_rank=2):
    zeros = (0,) * len(shape)
    index_map = (lambda s: zeros) if grid_rank == 1 else (lambda b, t: zeros)
    return pl.BlockSpec(shape, index_map, pipeline_mode=pl.Buffered(1))


def _block_diag(w):
    n, d, e = w.shape
    eye = jnp.eye(n, dtype=w.dtype)
    return jnp.einsum('nde,nm->ndme', w, eye).reshape(n * d, n * e)


def _mixer_call(layer, x, ng, win, rgw, rgb, wri, bri, lam, lbraw, scw, og, ones_bd, wout):
    batch, seq, d_model = x.shape
    tb = min(TIME_BLOCK, seq)
    assert seq % tb == 0 and tb % HG_CHUNK == 0
    nt = seq // tb
    steps = batch * nt
    row_spec = pl.BlockSpec((None, tb, d_model), lambda s: (s // nt, s % nt, 0))
    next_spec = pl.BlockSpec(
        (None, tb, d_model),
        lambda s: (jnp.minimum(s + 1, steps - 1) // nt, jnp.minimum(s + 1, steps - 1) % nt, 0))
    consts = (ng, win, rgw, rgb, wri, bri, lam, lbraw, scw, og, ones_bd, wout)
    return pl.pallas_call(
        functools.partial(_mixer_kernel, layer, nt),
        grid=(steps,),
        in_specs=[row_spec, next_spec] + [_const_spec(c.shape, 1) for c in consts],
        out_specs=row_spec,
        out_shape=jax.ShapeDtypeStruct(x.shape, x.dtype),
        scratch_shapes=[
            pltpu.VMEM(win.shape, BF16),
            pltpu.VMEM(wri.shape, BF16),
            pltpu.VMEM(wout.shape, BF16),
            pltpu.VMEM((tb, win.shape[1]), F32),
            pltpu.VMEM((tb, win.shape[1]), F32),
            pltpu.VMEM((tb + HALO, RG_WIDTH), F32),
            pltpu.VMEM((tb + HALO, SC_WIDTH), F32),
            pltpu.VMEM((HALO, RG_WIDTH), F32),
            pltpu.VMEM((HG_WIDTH // LANES, LANES, LANES), F32),
            pltpu.VMEM((tb, HG_WIDTH), F32),
        ],
        compiler_params=pltpu.CompilerParams(
            dimension_semantics=("arbitrary",), vmem_limit_bytes=VMEM_LIMIT_BYTES),
        name=f"mixer_l{layer}",
    )(x, x, *consts)


def _ffn_call(layer, last, x, p, ng, wup, cw, wdown, wgate, wproj, fg):
    batch, seq, d_model = x.shape
    tb = min(TIME_BLOCK, seq)
    assert seq % tb == 0
    row_spec = pl.BlockSpec((None, tb, d_model), lambda b, t: (b, t, 0))
    p_spec = pl.BlockSpec((None, tb, p.shape[-1]), lambda b, t: (b, t, 0))
    consts = (ng, wup, cw, wdown, wgate, wproj, fg)
    return pl.pallas_call(
        functools.partial(_ffn_kernel, last),
        grid=(batch, seq // tb),
        in_specs=[row_spec, p_spec] + [_const_spec(c.shape) for c in consts],
        out_specs=row_spec,
        out_shape=jax.ShapeDtypeStruct(x.shape, x.dtype),
        scratch_shapes=[pltpu.VMEM((tb + HALO, wup.shape[1]), F32)],
        compiler_params=pltpu.CompilerParams(
            dimension_semantics=("arbitrary", "arbitrary"), vmem_limit_bytes=VMEM_LIMIT_BYTES),
        name=f"ffn_l{layer}",
    )(x, p, *consts)


def kernel(x, p, norm_mix_gain, w_in, rg_conv_w, rg_conv_b, rg_w_r, rg_b_r, rg_w_i, rg_b_i, rg_lambda, hg_lower_bounds, sc_conv_w, mix_out_gain, w_out, norm_ffn_gain, ffn_w_up, ffn_conv_w, ffn_w_down, ple_w_proj, ple_w_gate, final_norm_gain):
    depth = w_in.shape[0]
    head_id = jnp.arange(RG_WIDTH) // HEAD_DIM
    ones_bd = (head_id[:, None] == head_id[None, :]).astype(BF16)
    row = lambda v: v.reshape(1, -1)
    for i in range(depth):
        wri = jnp.concatenate([_block_diag(rg_w_r[i]), _block_diag(rg_w_i[i])], axis=1)
        bri = jnp.concatenate([rg_b_r[i].reshape(-1), rg_b_i[i].reshape(-1)]).reshape(1, -1)
        x = _mixer_call(i, x, row(norm_mix_gain[i]), w_in[i], rg_conv_w[i],
                        row(rg_conv_b[i]), wri, bri, row(rg_lambda[i]), hg_lower_bounds,
                        sc_conv_w[i], row(mix_out_gain[i]), ones_bd, w_out[i])
        x = _ffn_call(i, i == depth - 1, x, p[i], row(norm_ffn_gain[i]), ffn_w_up[i].astype(BF16),
                      ffn_conv_w[i], ffn_w_down[i].astype(BF16), ple_w_gate[i].astype(BF16),
                      ple_w_proj[i].astype(BF16), row(final_norm_gain))
    return x
```

```python
import functools

import jax
import jax.numpy as jnp
from jax import lax
from jax.experimental import pallas as pl
from jax.experimental.pallas import tpu as pltpu

HEAD_DIM = 64
RG_WIDTH = 384
HG_WIDTH = 384
SC_WIDTH = 256
RG_CONV = 4
SC_CONV = 3
FFN_CONV = 3
RG_C = 8.0
HG_CHUNK = 64
EPS = 1e-6
SUBLANES = 8
LANES = 128
HALO = SUBLANES
TIME_BLOCK = 256
FFN_TIME_BLOCK = 512
PROJ_TILE = 256
WEIGHT_CAST_ROWS = 128
VMEM_LIMIT_BYTES = 56 * 1024 * 1024
MATMUL_LEVEL_HALVES = (32, 16, 8, 4, 2)

_sizes = (RG_WIDTH, RG_WIDTH, HG_WIDTH, HG_WIDTH, HG_WIDTH, HG_WIDTH, SC_WIDTH, SC_WIDTH, SC_WIDTH)
PROJ_OFFSETS = tuple(sum(_sizes[:i]) for i in range(len(_sizes) + 1))

F32 = jnp.float32
BF16 = jnp.bfloat16


def _dot(a, b):
    return jnp.dot(a.astype(BF16), b.astype(BF16), preferred_element_type=F32)


def _dot_nt(a, b):
    return lax.dot_general(a.astype(BF16), b.astype(BF16), (((1,), (1,)), ((), ())),
                           preferred_element_type=F32)


def _dot_tn(a, b):
    return lax.dot_general(a.astype(BF16), b.astype(BF16), (((0,), (0,)), ((), ())),
                           preferred_element_type=F32)


def _rmsnorm(x, gain):
    ms = jnp.mean(x * x, axis=-1, keepdims=True)
    return x * lax.rsqrt(ms + EPS) * gain


def _head_rmsnorm(y, ones_bd):
    ssum = _dot(y * y, ones_bd)
    return y * lax.rsqrt(ssum * (1.0 / HEAD_DIM) + EPS)


def _causal_conv(ext_ref, new_rows, w_ref, taps, rows):
    ext_ref[HALO:HALO + rows, :] = new_rows
    out = None
    for k in range(taps):
        term = w_ref[k:k + 1, :] * ext_ref[pl.ds(HALO - (taps - 1) + k, rows), :]
        out = term if out is None else out + term
    ext_ref[0:HALO, :] = ext_ref[rows:rows + HALO, :]
    return out


def _tile_roll(x, shift):
    rows, w = x.shape
    x3 = x.reshape(rows // SUBLANES, SUBLANES, w)
    return pltpu.roll(x3, shift, axis=1).reshape(rows, w)


def _tile_pos(shape):
    return lax.broadcasted_iota(jnp.int32, shape, 0) % SUBLANES


def _linear_scan(a, b, h0):
    rows = a.shape[0]
    pos = _tile_pos(a.shape)
    s = 1
    while s < SUBLANES:
        keep = pos >= s
        b = jnp.where(keep, a * _tile_roll(b, s) + b, b)
        a = jnp.where(keep, a * _tile_roll(a, s), a)
        s *= 2
    tiles = []
    for t in range(rows // SUBLANES):
        sl = slice(t * SUBLANES, (t + 1) * SUBLANES)
        h = a[sl] * h0 + b[sl]
        h0 = h[SUBLANES - 1:SUBLANES, :]
        tiles.append(h)
    return jnp.concatenate(tiles, axis=0)


def _chunk_cumsum(x, chunk):
    rows = x.shape[0]
    pos = _tile_pos(x.shape)
    s = 1
    while s < SUBLANES:
        x = jnp.where(pos >= s, x + _tile_roll(x, s), x)
        s *= 2
    tiles = []
    last = None
    for t in range(rows // SUBLANES):
        blk = x[t * SUBLANES:(t + 1) * SUBLANES]
        if t % (chunk // SUBLANES) != 0:
            blk = blk + last
        last = blk[SUBLANES - 1:SUBLANES, :]
        tiles.append(blk)
    return jnp.concatenate(tiles, axis=0)


def _block_row_broadcast(g, block, row):
    pieces = [jnp.broadcast_to(g[b * block + row:b * block + row + 1, :], (block, g.shape[1]))
              for b in range(g.shape[0] // block)]
    return jnp.concatenate(pieces, axis=0)


def _level_weight(g, f, half):
    if half >= 4:
        return jnp.exp(-jnp.abs(g - _block_row_broadcast(g, 2 * half, half - 1)))
    assert half == 2
    pos = lax.broadcasted_iota(jnp.int32, f.shape, 0) % (2 * half)
    nxt = _tile_roll(f, SUBLANES - 1)
    prv = _tile_roll(f, 1)
    return jnp.where(pos == 0, nxt, jnp.where(pos == 1, 1.0, jnp.where(pos == 2, f, f * prv)))


def _mixer_kernel(layer, steps_per_seq, x_ref, xn_ref, ng_ref, win_f32, rgw_ref, rgb_ref, wri_f32,
                  bri_ref, lam_ref, lbraw_ref, scw_ref, og_ref, ones_ref, wout_f32, o_ref,
                  win_ref, wri_ref, wout_ref, z_even, z_odd, ext_a, ext_c, h_carry, st_ref, o_s):
    step = pl.program_id(0)

    @pl.when(step == 0)
    def _():
        for src, dst in ((win_f32, win_ref), (wri_f32, wri_ref), (wout_f32, wout_ref)):
            for r0 in range(0, src.shape[0], WEIGHT_CAST_ROWS):
                rs = slice(r0, min(r0 + WEIGHT_CAST_ROWS, src.shape[0]))
                dst[rs, :] = src[rs, :].astype(BF16)

    def projection_tiles(src_ref, dst_ref):
        hn = _rmsnorm(src_ref[...], ng_ref[...]).astype(BF16)

        def tile(j):
            cols = slice(j * PROJ_TILE, (j + 1) * PROJ_TILE)
            dst_ref[:, cols] = jnp.dot(hn, win_ref[:, cols], preferred_element_type=F32)

        return [functools.partial(tile, j) for j in range(win_ref.shape[1] // PROJ_TILE)]

    @pl.when(step == 0)
    def _():
        for tile in projection_tiles(x_ref, z_even):
            tile()

    @pl.when(step % steps_per_seq == 0)
    def _():
        ext_a[0:HALO, :] = jnp.zeros((HALO, RG_WIDTH), F32)
        ext_c[0:HALO, :] = jnp.zeros((HALO, SC_WIDTH), F32)
        h_carry[...] = jnp.zeros(h_carry.shape, F32)
        st_ref[...] = jnp.zeros(st_ref.shape, F32)

    mix = functools.partial(
        _mixer_step, layer, x_ref, rgw_ref=rgw_ref, rgb_ref=rgb_ref, wri_ref=wri_ref,
        bri_ref=bri_ref, lam_ref=lam_ref, lbraw_ref=lbraw_ref, scw_ref=scw_ref, og_ref=og_ref,
        ones_ref=ones_ref, wout_ref=wout_ref, o_ref=o_ref, ext_a=ext_a, ext_c=ext_c,
        h_carry=h_carry, st_ref=st_ref, o_s=o_s)

    @pl.when(step % 2 == 0)
    def _():
        mix(z_even, projection_tiles(xn_ref, z_odd))

    @pl.when(step % 2 == 1)
    def _():
        mix(z_odd, projection_tiles(xn_ref, z_even))


def _mixer_step(layer, x_ref, z_ref, side_work, *, rgw_ref, rgb_ref, wri_ref, bri_ref, lam_ref,
                lbraw_ref, scw_ref, og_ref, ones_ref, wout_ref, o_ref, ext_a, ext_c, h_carry,
                st_ref, o_s):
    side_work = list(side_work)

    def interleave(n=1):
        for _ in range(n):
            if side_work:
                side_work.pop(0)()

    rows = x_ref.shape[0]
    x = x_ref[...]
    a_u, a_gate, b_q, b_f, b_v, b_g, c_b, c_c, c_v = (
        z_ref[:, PROJ_OFFSETS[i]:PROJ_OFFSETS[i + 1]] for i in range(9))
    ones_bd = ones_ref[...]
    gains = og_ref[...]

    u = _causal_conv(ext_a, a_u, rgw_ref, RG_CONV, rows) + rgb_ref[...]
    ri = _dot(u, wri_ref[...]) + bri_ref[...]
    interleave(2)
    r = jax.nn.sigmoid(ri[:, :RG_WIDTH])
    ig = jax.nn.sigmoid(ri[:, RG_WIDTH:])
    log_a = -RG_C * r * jax.nn.softplus(-lam_ref[...])
    a = jnp.exp(log_a)
    th = jnp.tanh(log_a)
    beta = jnp.sqrt(-2.0 * th / (1.0 - th))
    h = _linear_scan(a, beta * ig * u, h_carry[0:1, :])
    h_carry[...] = jnp.broadcast_to(h[rows - 1:rows, :], h_carry.shape)
    interleave(2)
    y_a = _head_rmsnorm(h * jax.nn.gelu(a_gate), ones_bd) * gains[:, 0:RG_WIDTH]
    interleave(1)

    raw = lbraw_ref[...]
    e = jnp.exp(raw - jnp.max(raw, axis=0, keepdims=True))
    sm = e / jnp.sum(e, axis=0, keepdims=True)
    lb = sm[0:1, :] - sm[0:1, :]
    for j in range(1, layer + 1):
        lb = lb + sm[j:j + 1, :]
    f = lb + (1.0 - lb) * jax.nn.sigmoid(b_f)
    qh = jax.nn.silu(b_q) * (HEAD_DIM ** -0.5)
    kk = 1.0 - f
    g = _chunk_cumsum(jnp.log(f), HG_CHUNK)
    g_last = _block_row_broadcast(g, HG_CHUNK, HG_CHUNK - 1)
    q_inter = (qh * jnp.exp(g)).astype(BF16)
    k_state = (kk * jnp.exp(g_last - g)).astype(BF16)
    v_b = b_v.astype(BF16)
    band_e = [(qh * kk).astype(BF16), (qh * f * _tile_roll(kk, 1)).astype(BF16)]
    level_q, level_k = [], []
    for hf in MATMUL_LEVEL_HALVES:
        w = _level_weight(g, f, hf)
        level_q.append((qh * w).astype(BF16))
        level_k.append((kk * w).astype(BF16))
    interleave(1)

    ti = lax.broadcasted_iota(jnp.int32, (2 * HG_CHUNK, HEAD_DIM), 0) % HG_CHUNK
    si = lax.broadcasted_iota(jnp.int32, (2 * HG_CHUNK, HEAD_DIM), 1)
    top = ti ^ si
    band_masks = [ti == si, (top == 1) & (ti > si)]
    level_masks = [(top >= hf) & (top < 2 * hf) & (ti > si) for hf in MATMUL_LEVEL_HALVES]
    lane = lax.broadcasted_iota(jnp.int32, (HG_CHUNK, LANES), 1)
    head0 = lane < HEAD_DIM
    sr = lax.broadcasted_iota(jnp.int32, (LANES, LANES), 0) // HEAD_DIM
    sc = lax.broadcasted_iota(jnp.int32, (LANES, LANES), 1) // HEAD_DIM
    same_head = sr == sc
    head_ones = jnp.ones((LANES, HEAD_DIM), BF16)

    def head_stack(m):
        zero = jnp.zeros_like(m)
        return jnp.concatenate([jnp.where(head0, m, zero), jnp.where(head0, zero, m)], axis=0)

    pairs = HG_WIDTH // LANES
    states = [st_ref[p] for p in range(pairs)]
    def chunk_scores(c, p):
        rs = slice(c * HG_CHUNK, (c + 1) * HG_CHUNK)
        ls = slice(p * LANES, (p + 1) * LANES)
        scores = jnp.zeros((2 * HG_CHUNK, HEAD_DIM), F32)
        for mask, be in zip(band_masks, band_e):
            scores = jnp.where(mask, _dot(head_stack(be[rs, ls]), head_ones), scores)
        for mask, lq, lk in zip(level_masks, level_q, level_k):
            scores = jnp.where(mask, _dot_nt(head_stack(lq[rs, ls]), lk[rs, ls]), scores)
        return scores.astype(BF16)

    def chunk_output(c, p, scores):
        rs = slice(c * HG_CHUNK, (c + 1) * HG_CHUNK)
        ls = slice(p * LANES, (p + 1) * LANES)
        o_inter = _dot_nt(q_inter[rs, ls], states[p])
        states[p] = (states[p] * jnp.exp(g_last[c * HG_CHUNK:c * HG_CHUNK + 1, ls])
                     + jnp.where(same_head, _dot_tn(v_b[rs, ls], k_state[rs, ls]), 0.0))
        o_stack = _dot(scores, v_b[rs, ls])
        o_s[rs, ls] = o_inter + jnp.where(head0, o_stack[:HG_CHUNK], o_stack[HG_CHUNK:])

    pending = None
    for c in range(rows // HG_CHUNK):
        for p in range(pairs):
            scores = chunk_scores(c, p)
            if pending is not None:
                chunk_output(*pending)
            pending = (c, p, scores)
            interleave()
    chunk_output(*pending)
    for p in range(pairs):
        st_ref[p] = states[p]
    y_b = (_head_rmsnorm(o_s[...], ones_bd) * gains[:, RG_WIDTH:RG_WIDTH + HG_WIDTH]
           * jax.nn.silu(b_g))

    conv = _causal_conv(ext_c, c_c * c_v, scw_ref, SC_CONV, rows)
    y_c = (_head_rmsnorm(c_b * conv, ones_bd[:SC_WIDTH, :SC_WIDTH])
           * gains[:, RG_WIDTH + HG_WIDTH:])

    interleave(len(side_work))
    y = jnp.concatenate([y_a, y_b, y_c], axis=-1)
    o_ref[...] = x + _dot(y, wout_ref[...])


def _ffn_kernel(last, x_ref, p_ref, ng_ref, wup_ref, cw_ref, wdown_ref, wgate_ref, wproj_ref,
                fg_ref, o_ref, ext):
    rows = x_ref.shape[0]
    d_ff = wdown_ref.shape[0]

    @pl.when(pl.program_id(1) == 0)
    def _():
        ext[0:HALO, :] = jnp.zeros((HALO, ext.shape[1]), F32)

    x = x_ref[...]
    gu = _causal_conv(ext, _dot(_rmsnorm(x, ng_ref[...]), wup_ref[...]), cw_ref, FFN_CONV, rows)
    act = jax.nn.silu(gu[:, :d_ff]) * gu[:, d_ff:]
    x = x + _dot(act, wdown_ref[...])
    gate = jax.nn.sigmoid(_dot(x, wgate_ref[...]))
    x = x + gate * _dot(p_ref[...], wproj_ref[...])
    if last:
        x = _rmsnorm(x, fg_ref[...])
    o_ref[...] = x


def _const_spec(shape, grid_rank=2):
    zeros = (0,) * len(shape)
    index_map = (lambda s: zeros) if grid_rank == 1 else (lambda b, t: zeros)
    return pl.BlockSpec(shape, index_map, pipeline_mode=pl.Buffered(1))


def _layer_spec(stacked, layer, grid_rank=2):
    shape = stacked.shape[1:]
    index = (layer,) + (0,) * len(shape)
    index_map = (lambda s: index) if grid_rank == 1 else (lambda b, t: index)
    return pl.BlockSpec((None,) + shape, index_map, pipeline_mode=pl.Buffered(1))


def _block_diag(w):
    n, d, e = w.shape
    eye = jnp.eye(n, dtype=w.dtype)
    return jnp.einsum('nde,nm->ndme', w, eye).reshape(n * d, n * e)


def _mixer_call(layer, x, ng, win, rgw, rgb, wri, bri, lam, lbraw, scw, og, ones_bd, wout):
    batch, seq, d_model = x.shape
    tb = min(TIME_BLOCK, seq)
    assert seq % tb == 0 and tb % HG_CHUNK == 0
    nt = seq // tb
    steps = batch * nt
    row_spec = pl.BlockSpec((None, tb, d_model), lambda s: (s // nt, s % nt, 0))
    next_spec = pl.BlockSpec(
        (None, tb, d_model),
        lambda s: (jnp.minimum(s + 1, steps - 1) // nt, jnp.minimum(s + 1, steps - 1) % nt, 0))
    consts = (ng, win, rgw, rgb, wri, bri, lam, lbraw, scw, og, ones_bd, wout)
    specs = [_layer_spec(c, layer, 1) if c is win or c is wout else _const_spec(c.shape, 1)
             for c in consts]
    return pl.pallas_call(
        functools.partial(_mixer_kernel, layer, nt),
        grid=(steps,),
        in_specs=[row_spec, next_spec] + specs,
        out_specs=row_spec,
        out_shape=jax.ShapeDtypeStruct(x.shape, x.dtype),
        scratch_shapes=[
            pltpu.VMEM(win.shape[1:], BF16),
            pltpu.VMEM(wri.shape, BF16),
            pltpu.VMEM(wout.shape[1:], BF16),
            pltpu.VMEM((tb, win.shape[2]), F32),
            pltpu.VMEM((tb, win.shape[2]), F32),
            pltpu.VMEM((tb + HALO, RG_WIDTH), F32),
            pltpu.VMEM((tb + HALO, SC_WIDTH), F32),
            pltpu.VMEM((HALO, RG_WIDTH), F32),
            pltpu.VMEM((HG_WIDTH // LANES, LANES, LANES), F32),
            pltpu.VMEM((tb, HG_WIDTH), F32),
        ],
        compiler_params=pltpu.CompilerParams(
            dimension_semantics=("arbitrary",), vmem_limit_bytes=VMEM_LIMIT_BYTES),
        name=f"mixer_l{layer}",
    )(x, x, *consts)


def _ffn_call(layer, last, x, p, ng, wup, cw, wdown, wgate, wproj, fg):
    batch, seq, d_model = x.shape
    tb = min(FFN_TIME_BLOCK, seq)
    assert seq % tb == 0
    row_spec = pl.BlockSpec((None, tb, d_model), lambda b, t: (b, t, 0))
    p_spec = pl.BlockSpec((None, None, tb, p.shape[-1]), lambda b, t: (layer, b, t, 0))
    consts = (ng, wup, cw, wdown, wgate, wproj, fg)
    stacked = (wup, wdown, wgate, wproj)
    specs = [_layer_spec(c, layer) if any(c is s for s in stacked) else _const_spec(c.shape)
             for c in consts]
    return pl.pallas_call(
        functools.partial(_ffn_kernel, last),
        grid=(batch, seq // tb),
        in_specs=[row_spec, p_spec] + specs,
        out_specs=row_spec,
        out_shape=jax.ShapeDtypeStruct(x.shape, x.dtype),
        scratch_shapes=[pltpu.VMEM((tb + HALO, wup.shape[2]), F32)],
        compiler_params=pltpu.CompilerParams(
            dimension_semantics=("arbitrary", "arbitrary"), vmem_limit_bytes=VMEM_LIMIT_BYTES),
        name=f"ffn_l{layer}",
    )(x, p, *consts)


def kernel(x, p, norm_mix_gain, w_in, rg_conv_w, rg_conv_b, rg_w_r, rg_b_r, rg_w_i, rg_b_i, rg_lambda, hg_lower_bounds, sc_conv_w, mix_out_gain, w_out, norm_ffn_gain, ffn_w_up, ffn_conv_w, ffn_w_down, ple_w_proj, ple_w_gate, final_norm_gain):
    depth = w_in.shape[0]
    head_id = jnp.arange(RG_WIDTH) // HEAD_DIM
    ones_bd = (head_id[:, None] == head_id[None, :]).astype(BF16)
    row = lambda v: v.reshape(1, -1)
    ffn_weights = [w.astype(BF16) for w in (ffn_w_up, ffn_w_down, ple_w_gate, ple_w_proj)]
    for i in range(depth):
        wri = jnp.concatenate([_block_diag(rg_w_r[i]), _block_diag(rg_w_i[i])], axis=1)
        bri = jnp.concatenate([rg_b_r[i].reshape(-1), rg_b_i[i].reshape(-1)]).reshape(1, -1)
        x = _mixer_call(i, x, row(norm_mix_gain[i]), w_in, rg_conv_w[i],
                        row(rg_conv_b[i]), wri, bri, row(rg_lambda[i]), hg_lower_bounds,
                        sc_conv_w[i], row(mix_out_gain[i]), ones_bd, w_out)
        wup, wdown, wgate, wproj = ffn_weights
        x = _ffn_call(i, i == depth - 1, x, p, row(norm_ffn_gain[i]), wup, ffn_conv_w[i], wdown,
                      wgate, wproj, row(final_norm_gain))
    return x
```

```python
import functools

import jax
import jax.numpy as jnp
from jax import lax
from jax.experimental import pallas as pl
from jax.experimental.pallas import tpu as pltpu

HEAD_DIM = 64
RG_WIDTH = 384
HG_WIDTH = 384
SC_WIDTH = 256
RG_CONV = 4
SC_CONV = 3
FFN_CONV = 3
RG_C = 8.0
HG_CHUNK = 64
EPS = 1e-6
LOG2_E = 1.4426950408889634
SUBLANES = 8
LANES = 128
HALO = SUBLANES
TIME_BLOCK = 256
FFN_TIME_BLOCK = 512
PROJ_TILE = 256
WEIGHT_CAST_ROWS = 128
VMEM_LIMIT_BYTES = 56 * 1024 * 1024
MATMUL_LEVEL_HALVES = (32, 16, 8, 4, 2)

_sizes = (RG_WIDTH, RG_WIDTH, HG_WIDTH, HG_WIDTH, HG_WIDTH, HG_WIDTH, SC_WIDTH, SC_WIDTH, SC_WIDTH)
PROJ_OFFSETS = tuple(sum(_sizes[:i]) for i in range(len(_sizes) + 1))

F32 = jnp.float32
BF16 = jnp.bfloat16


def _dot(a, b):
    return jnp.dot(a.astype(BF16), b.astype(BF16), preferred_element_type=F32)


def _dot_nt(a, b):
    return lax.dot_general(a.astype(BF16), b.astype(BF16), (((1,), (1,)), ((), ())),
                           preferred_element_type=F32)


def _dot_tn(a, b):
    return lax.dot_general(a.astype(BF16), b.astype(BF16), (((0,), (0,)), ((), ())),
                           preferred_element_type=F32)


def _rmsnorm(x, gain):
    ms = jnp.mean(x * x, axis=-1, keepdims=True)
    return x * lax.rsqrt(ms + EPS) * gain


def _head_rmsnorm(y, ones_bd):
    ssum = _dot(y * y, ones_bd)
    return y * lax.rsqrt(ssum * (1.0 / HEAD_DIM) + EPS)


def _causal_conv(ext_ref, new_rows, w_ref, taps, rows):
    ext_ref[HALO:HALO + rows, :] = new_rows
    out = None
    for k in range(taps):
        term = w_ref[k:k + 1, :] * ext_ref[pl.ds(HALO - (taps - 1) + k, rows), :]
        out = term if out is None else out + term
    ext_ref[0:HALO, :] = ext_ref[rows:rows + HALO, :]
    return out


def _tile_roll(x, shift):
    rows, w = x.shape
    x3 = x.reshape(rows // SUBLANES, SUBLANES, w)
    return pltpu.roll(x3, shift, axis=1).reshape(rows, w)


def _tile_pos(shape):
    return lax.broadcasted_iota(jnp.int32, shape, 0) % SUBLANES


def _linear_scan(a, b, h0):
    rows = a.shape[0]
    pos = _tile_pos(a.shape)
    s = 1
    while s < SUBLANES:
        keep = pos >= s
        b = jnp.where(keep, a * _tile_roll(b, s) + b, b)
        a = jnp.where(keep, a * _tile_roll(a, s), a)
        s *= 2
    tiles = []
    for t in range(rows // SUBLANES):
        sl = slice(t * SUBLANES, (t + 1) * SUBLANES)
        h = a[sl] * h0 + b[sl]
        h0 = h[SUBLANES - 1:SUBLANES, :]
        tiles.append(h)
    return jnp.concatenate(tiles, axis=0)


def _chunk_cumsum(x, chunk):
    rows = x.shape[0]
    pos = _tile_pos(x.shape)
    s = 1
    while s < SUBLANES:
        x = jnp.where(pos >= s, x + _tile_roll(x, s), x)
        s *= 2
    tiles = []
    last = None
    for t in range(rows // SUBLANES):
        blk = x[t * SUBLANES:(t + 1) * SUBLANES]
        if t % (chunk // SUBLANES) != 0:
            blk = blk + last
        last = blk[SUBLANES - 1:SUBLANES, :]
        tiles.append(blk)
    return jnp.concatenate(tiles, axis=0)


def _block_row_broadcast(g, block, row):
    pieces = [jnp.broadcast_to(g[b * block + row:b * block + row + 1, :], (block, g.shape[1]))
              for b in range(g.shape[0] // block)]
    return jnp.concatenate(pieces, axis=0)


def _level_weight(g, f, half):
    if half >= 4:
        return jnp.exp2(-jnp.abs(g - _block_row_broadcast(g, 2 * half, half - 1)))
    assert half == 2
    pos = lax.broadcasted_iota(jnp.int32, f.shape, 0) % (2 * half)
    nxt = _tile_roll(f, SUBLANES - 1)
    prv = _tile_roll(f, 1)
    return jnp.where(pos == 0, nxt, jnp.where(pos == 1, 1.0, jnp.where(pos == 2, f, f * prv)))


def _mixer_kernel(layer, steps_per_seq, x_ref, xn_ref, ng_ref, win_f32, rgw_ref, rgb_ref, wri_f32,
                  bri_ref, lam_ref, lbraw_ref, scw_ref, og_ref, ones_ref, wout_f32, o_ref,
                  win_ref, wri_ref, wout_ref, z_first, z_second, ext_a, ext_c, h_carry, st_ref,
                  o_first, o_second):
    step = pl.program_id(0)
    tb = xn_ref.shape[0]
    first, second = slice(0, tb), slice(tb, 2 * tb)

    @pl.when(step == 0)
    def _():
        for src, dst in ((win_f32, win_ref), (wri_f32, wri_ref), (wout_f32, wout_ref)):
            for r0 in range(0, src.shape[0], WEIGHT_CAST_ROWS):
                rs = slice(r0, min(r0 + WEIGHT_CAST_ROWS, src.shape[0]))
                dst[rs, :] = src[rs, :].astype(BF16)

    def projection_tiles(src_ref, src_rows, dst_ref):
        hn = _rmsnorm(src_ref[src_rows, :], ng_ref[...]).astype(BF16)

        def tile(j):
            cols = slice(j * PROJ_TILE, (j + 1) * PROJ_TILE)
            dst_ref[:, cols] = jnp.dot(hn, win_ref[:, cols], preferred_element_type=F32)

        return [functools.partial(tile, j) for j in range(win_ref.shape[1] // PROJ_TILE)]

    @pl.when(step == 0)
    def _():
        for tile in projection_tiles(x_ref, first, z_first):
            tile()

    @pl.when(step % steps_per_seq == 0)
    def _():
        ext_a[0:HALO, :] = jnp.zeros((HALO, RG_WIDTH), F32)
        ext_c[0:HALO, :] = jnp.zeros((HALO, SC_WIDTH), F32)
        h_carry[...] = jnp.zeros(h_carry.shape, F32)
        st_ref[...] = jnp.zeros(st_ref.shape, F32)

    mix = functools.partial(
        _mixer_step, layer, x_ref, rgw_ref=rgw_ref, rgb_ref=rgb_ref, wri_ref=wri_ref,
        bri_ref=bri_ref, lam_ref=lam_ref, lbraw_ref=lbraw_ref, scw_ref=scw_ref, og_ref=og_ref,
        ones_ref=ones_ref, wout_ref=wout_ref, o_ref=o_ref, ext_a=ext_a, ext_c=ext_c,
        h_carry=h_carry, st_ref=st_ref)
    mix(first, z_first, projection_tiles(x_ref, second, z_second), o_first)
    mix(second, z_second, projection_tiles(xn_ref, slice(None), z_first), o_second)


def _mixer_step(layer, x_ref, block_rows, z_ref, side_work, o_s, *, rgw_ref, rgb_ref, wri_ref,
                bri_ref, lam_ref, lbraw_ref, scw_ref, og_ref, ones_ref, wout_ref, o_ref, ext_a,
                ext_c, h_carry, st_ref):
    side_work = list(side_work)

    def interleave(n=1):
        for _ in range(n):
            if side_work:
                side_work.pop(0)()

    rows = z_ref.shape[0]
    x = x_ref[block_rows, :]
    a_u, a_gate, b_q, b_f, b_v, b_g, c_b, c_c, c_v = (
        z_ref[:, PROJ_OFFSETS[i]:PROJ_OFFSETS[i + 1]] for i in range(9))
    ones_bd = ones_ref[...]
    gains = og_ref[...]

    u = _causal_conv(ext_a, a_u, rgw_ref, RG_CONV, rows) + rgb_ref[...]
    ri = _dot(u, wri_ref[...]) + bri_ref[...]
    interleave(2)
    r = jax.nn.sigmoid(ri[:, :RG_WIDTH])
    ig = jax.nn.sigmoid(ri[:, RG_WIDTH:])
    decay_rate = -RG_C * jax.nn.softplus(-lam_ref[...])
    log_a = r * decay_rate
    a = jnp.exp2(r * (decay_rate * LOG2_E))
    th = jnp.tanh(log_a)
    beta = jnp.sqrt(-2.0 * th / (1.0 - th))
    h = _linear_scan(a, beta * ig * u, h_carry[0:1, :])
    h_carry[...] = jnp.broadcast_to(h[rows - 1:rows, :], h_carry.shape)
    interleave(2)
    y_a = _head_rmsnorm(h * jax.nn.gelu(a_gate), ones_bd) * gains[:, 0:RG_WIDTH]
    interleave(1)

    raw = lbraw_ref[...]
    e = jnp.exp(raw - jnp.max(raw, axis=0, keepdims=True))
    sm = e / jnp.sum(e, axis=0, keepdims=True)
    lb = sm[0:1, :] - sm[0:1, :]
    for j in range(1, layer + 1):
        lb = lb + sm[j:j + 1, :]
    f = lb + (1.0 - lb) * jax.nn.sigmoid(b_f)
    qh = jax.nn.silu(b_q) * (HEAD_DIM ** -0.5)
    kk = 1.0 - f
    g = _chunk_cumsum(jnp.log2(f), HG_CHUNK)
    g_last = _block_row_broadcast(g, HG_CHUNK, HG_CHUNK - 1)
    q_inter = (qh * jnp.exp2(g)).astype(BF16)
    k_state = (kk * jnp.exp2(g_last - g)).astype(BF16)
    v_b = b_v.astype(BF16)
    band_e = [(qh * kk).astype(BF16), (qh * f * _tile_roll(kk, 1)).astype(BF16)]
    level_q, level_k = [], []
    for hf in MATMUL_LEVEL_HALVES:
        w = _level_weight(g, f, hf)
        level_q.append((qh * w).astype(BF16))
        level_k.append((kk * w).astype(BF16))
    interleave(1)

    ti = lax.broadcasted_iota(jnp.int32, (2 * HG_CHUNK, HEAD_DIM), 0) % HG_CHUNK
    si = lax.broadcasted_iota(jnp.int32, (2 * HG_CHUNK, HEAD_DIM), 1)
    top = ti ^ si
    band_masks = [ti == si, (top == 1) & (ti > si)]
    level_masks = [(top >= hf) & (top < 2 * hf) & (ti > si) for hf in MATMUL_LEVEL_HALVES]
    lane = lax.broadcasted_iota(jnp.int32, (HG_CHUNK, LANES), 1)
    head0 = lane < HEAD_DIM
    sr = lax.broadcasted_iota(jnp.int32, (LANES, LANES), 0) // HEAD_DIM
    sc = lax.broadcasted_iota(jnp.int32, (LANES, LANES), 1) // HEAD_DIM
    same_head = sr == sc
    head_ones = jnp.ones((LANES, HEAD_DIM), BF16)

    def head_stack(m):
        zero = jnp.zeros_like(m)
        return jnp.concatenate([jnp.where(head0, m, zero), jnp.where(head0, zero, m)], axis=0)

    pairs = HG_WIDTH // LANES
    states = [st_ref[p] for p in range(pairs)]
    def chunk_scores(c, p):
        rs = slice(c * HG_CHUNK, (c + 1) * HG_CHUNK)
        ls = slice(p * LANES, (p + 1) * LANES)
        scores = jnp.zeros((2 * HG_CHUNK, HEAD_DIM), F32)
        for mask, be in zip(band_masks, band_e):
            scores = jnp.where(mask, _dot(head_stack(be[rs, ls]), head_ones), scores)
        for mask, lq, lk in zip(level_masks, level_q, level_k):
            scores = jnp.where(mask, _dot_nt(head_stack(lq[rs, ls]), lk[rs, ls]), scores)
        return scores.astype(BF16)

    def chunk_output(c, p, scores):
        rs = slice(c * HG_CHUNK, (c + 1) * HG_CHUNK)
        ls = slice(p * LANES, (p + 1) * LANES)
        o_inter = _dot_nt(q_inter[rs, ls], states[p])
        states[p] = (states[p] * jnp.exp2(g_last[c * HG_CHUNK:c * HG_CHUNK + 1, ls])
                     + jnp.where(same_head, _dot_tn(v_b[rs, ls], k_state[rs, ls]), 0.0))
        o_stack = _dot(scores, v_b[rs, ls])
        o_s[rs, ls] = o_inter + jnp.where(head0, o_stack[:HG_CHUNK], o_stack[HG_CHUNK:])

    pending = None
    for c in range(rows // HG_CHUNK):
        for p in range(pairs):
            scores = chunk_scores(c, p)
            if pending is not None:
                chunk_output(*pending)
            pending = (c, p, scores)
            interleave()
    chunk_output(*pending)
    for p in range(pairs):
        st_ref[p] = states[p]
    y_b = (_head_rmsnorm(o_s[...], ones_bd) * gains[:, RG_WIDTH:RG_WIDTH + HG_WIDTH]
           * jax.nn.silu(b_g))

    conv = _causal_conv(ext_c, c_c * c_v, scw_ref, SC_CONV, rows)
    y_c = (_head_rmsnorm(c_b * conv, ones_bd[:SC_WIDTH, :SC_WIDTH])
           * gains[:, RG_WIDTH + HG_WIDTH:])

    interleave(len(side_work))
    y = jnp.concatenate([y_a, y_b, y_c], axis=-1)
    o_ref[block_rows, :] = x + _dot(y, wout_ref[...])


def _ffn_kernel(last, x_ref, p_ref, ng_ref, wup_ref, cw_ref, wdown_ref, wgate_ref, wproj_ref,
                fg_ref, o_ref, ext):
    rows = x_ref.shape[0]
    d_ff = wdown_ref.shape[0]

    @pl.when(pl.program_id(1) == 0)
    def _():
        ext[0:HALO, :] = jnp.zeros((HALO, ext.shape[1]), F32)

    x = x_ref[...]
    gu = _causal_conv(ext, _dot(_rmsnorm(x, ng_ref[...]), wup_ref[...]), cw_ref, FFN_CONV, rows)
    act = jax.nn.silu(gu[:, :d_ff]) * gu[:, d_ff:]
    x = x + _dot(act, wdown_ref[...])
    gate = jax.nn.sigmoid(_dot(x, wgate_ref[...]))
    x = x + gate * _dot(p_ref[...], wproj_ref[...])
    if last:
        x = _rmsnorm(x, fg_ref[...])
    o_ref[...] = x


def _const_spec(shape, grid_rank=2):
    zeros = (0,) * len(shape)
    index_map = (lambda s: zeros) if grid_rank == 1 else (lambda b, t: zeros)
    return pl.BlockSpec(shape, index_map, pipeline_mode=pl.Buffered(1))


def _layer_spec(stacked, layer, grid_rank=2):
    shape = stacked.shape[1:]
    index = (layer,) + (0,) * len(shape)
    index_map = (lambda s: index) if grid_rank == 1 else (lambda b, t: index)
    return pl.BlockSpec((None,) + shape, index_map, pipeline_mode=pl.Buffered(1))


def _block_diag(w):
    n, d, e = w.shape
    eye = jnp.eye(n, dtype=w.dtype)
    return jnp.einsum('nde,nm->ndme', w, eye).reshape(n * d, n * e)


def _mixer_call(layer, x, ng, win, rgw, rgb, wri, bri, lam, lbraw, scw, og, ones_bd, wout):
    batch, seq, d_model = x.shape
    tb = min(TIME_BLOCK, seq)
    assert seq % tb == 0 and tb % HG_CHUNK == 0
    assert seq % (2 * tb) == 0
    nt = seq // (2 * tb)
    steps = batch * nt
    row_spec = pl.BlockSpec((None, 2 * tb, d_model), lambda s: (s // nt, s % nt, 0))

    def next_block(s):
        nxt = jnp.minimum(s + 1, steps - 1)
        return (nxt // nt, 2 * (nxt % nt), 0)

    next_spec = pl.BlockSpec((None, tb, d_model), next_block)
    consts = (ng, win, rgw, rgb, wri, bri, lam, lbraw, scw, og, ones_bd, wout)
    specs = [_layer_spec(c, layer, 1) if c is win or c is wout else _const_spec(c.shape, 1)
             for c in consts]
    return pl.pallas_call(
        functools.partial(_mixer_kernel, layer, nt),
        grid=(steps,),
        in_specs=[row_spec, next_spec] + specs,
        out_specs=row_spec,
        out_shape=jax.ShapeDtypeStruct(x.shape, x.dtype),
        scratch_shapes=[
            pltpu.VMEM(win.shape[1:], BF16),
            pltpu.VMEM(wri.shape, BF16),
            pltpu.VMEM(wout.shape[1:], BF16),
            pltpu.VMEM((tb, win.shape[2]), F32),
            pltpu.VMEM((tb, win.shape[2]), F32),
            pltpu.VMEM((tb + HALO, RG_WIDTH), F32),
            pltpu.VMEM((tb + HALO, SC_WIDTH), F32),
            pltpu.VMEM((HALO, RG_WIDTH), F32),
            pltpu.VMEM((HG_WIDTH // LANES, LANES, LANES), F32),
            pltpu.VMEM((tb, HG_WIDTH), F32),
            pltpu.VMEM((tb, HG_WIDTH), F32),
        ],
        compiler_params=pltpu.CompilerParams(
            dimension_semantics=("arbitrary",), vmem_limit_bytes=VMEM_LIMIT_BYTES),
        name=f"mixer_l{layer}",
    )(x, x, *consts)


def _ffn_call(layer, last, x, p, ng, wup, cw, wdown, wgate, wproj, fg):
    batch, seq, d_model = x.shape
    tb = min(FFN_TIME_BLOCK, seq)
    assert seq % tb == 0
    row_spec = pl.BlockSpec((None, tb, d_model), lambda b, t: (b, t, 0))
    p_spec = pl.BlockSpec((None, None, tb, p.shape[-1]), lambda b, t: (layer, b, t, 0))
    consts = (ng, wup, cw, wdown, wgate, wproj, fg)
    stacked = (wup, wdown, wgate, wproj)
    specs = [_layer_spec(c, layer) if any(c is s for s in stacked) else _const_spec(c.shape)
             for c in consts]
    return pl.pallas_call(
        functools.partial(_ffn_kernel, last),
        grid=(batch, seq // tb),
        in_specs=[row_spec, p_spec] + specs,
        out_specs=row_spec,
        out_shape=jax.ShapeDtypeStruct(x.shape, x.dtype),
        scratch_shapes=[pltpu.VMEM((tb + HALO, wup.shape[2]), F32)],
        compiler_params=pltpu.CompilerParams(
            dimension_semantics=("arbitrary", "arbitrary"), vmem_limit_bytes=VMEM_LIMIT_BYTES),
        name=f"ffn_l{layer}",
    )(x, p, *consts)


def kernel(x, p, norm_mix_gain, w_in, rg_conv_w, rg_conv_b, rg_w_r, rg_b_r, rg_w_i, rg_b_i, rg_lambda, hg_lower_bounds, sc_conv_w, mix_out_gain, w_out, norm_ffn_gain, ffn_w_up, ffn_conv_w, ffn_w_down, ple_w_proj, ple_w_gate, final_norm_gain):
    depth = w_in.shape[0]
    head_id = jnp.arange(RG_WIDTH) // HEAD_DIM
    ones_bd = (head_id[:, None] == head_id[None, :]).astype(BF16)
    row = lambda v: v.reshape(1, -1)
    ffn_weights = [w.astype(BF16) for w in (ffn_w_up, ffn_w_down, ple_w_gate, ple_w_proj)]
    for i in range(depth):
        wri = jnp.concatenate([_block_diag(rg_w_r[i]), _block_diag(rg_w_i[i])], axis=1)
        bri = jnp.concatenate([rg_b_r[i].reshape(-1), rg_b_i[i].reshape(-1)]).reshape(1, -1)
        x = _mixer_call(i, x, row(norm_mix_gain[i]), w_in, rg_conv_w[i],
                        row(rg_conv_b[i]), wri, bri, row(rg_lambda[i]), hg_lower_bounds,
                        sc_conv_w[i], row(mix_out_gain[i]), ones_bd, w_out)
        wup, wdown, wgate, wproj = ffn_weights
        x = _ffn_call(i, i == depth - 1, x, p, row(norm_ffn_gain[i]), wup, ffn_conv_w[i], wdown,
                      wgate, wproj, row(final_norm_gain))
    return x
```

```python
import functools

import jax
import jax.numpy as jnp
from jax import lax
from jax.experimental import pallas as pl
from jax.experimental.pallas import tpu as pltpu

HEAD_DIM = 64
RG_WIDTH = 384
HG_WIDTH = 384
SC_WIDTH = 256
RG_CONV = 4
SC_CONV = 3
FFN_CONV = 3
RG_C = 8.0
HG_CHUNK = 64
EPS = 1e-6
LOG2_E = 1.4426950408889634
SUBLANES = 8
LANES = 128
HALO = SUBLANES
TIME_BLOCK = 256
FFN_TIME_BLOCK = 512
PROJ_TILE = 256
WEIGHT_CAST_ROWS = 128
RG_ROW_CHUNK = 16
VMEM_LIMIT_BYTES = 56 * 1024 * 1024
MATMUL_LEVEL_HALVES = (32, 16, 8, 4, 2)

_sizes = (RG_WIDTH, RG_WIDTH, HG_WIDTH, HG_WIDTH, HG_WIDTH, HG_WIDTH, SC_WIDTH, SC_WIDTH, SC_WIDTH)
PROJ_OFFSETS = tuple(sum(_sizes[:i]) for i in range(len(_sizes) + 1))

F32 = jnp.float32
BF16 = jnp.bfloat16


def _dot(a, b):
    return jnp.dot(a.astype(BF16), b.astype(BF16), preferred_element_type=F32)


def _dot_nt(a, b):
    return lax.dot_general(a.astype(BF16), b.astype(BF16), (((1,), (1,)), ((), ())),
                           preferred_element_type=F32)


def _dot_tn(a, b):
    return lax.dot_general(a.astype(BF16), b.astype(BF16), (((0,), (0,)), ((), ())),
                           preferred_element_type=F32)


def _rmsnorm(x, gain):
    ms = jnp.mean(x * x, axis=-1, keepdims=True)
    return x * lax.rsqrt(ms + EPS) * gain


def _head_rmsnorm(y, ones_bd):
    ssum = _dot(y * y, ones_bd)
    return y * lax.rsqrt(ssum * (1.0 / HEAD_DIM) + EPS)


def _causal_conv(ext_ref, new_rows, w_ref, taps, rows):
    ext_ref[HALO:HALO + rows, :] = new_rows
    out = None
    for k in range(taps):
        term = w_ref[k:k + 1, :] * ext_ref[pl.ds(HALO - (taps - 1) + k, rows), :]
        out = term if out is None else out + term
    ext_ref[0:HALO, :] = ext_ref[rows:rows + HALO, :]
    return out


def _tile_roll(x, shift):
    rows, w = x.shape
    x3 = x.reshape(rows // SUBLANES, SUBLANES, w)
    return pltpu.roll(x3, shift, axis=1).reshape(rows, w)


def _tile_pos(shape):
    return lax.broadcasted_iota(jnp.int32, shape, 0) % SUBLANES


def _linear_scan(a, b, h0):
    rows = a.shape[0]
    pos = _tile_pos(a.shape)
    s = 1
    while s < SUBLANES:
        keep = pos >= s
        b = jnp.where(keep, a * _tile_roll(b, s) + b, b)
        a = jnp.where(keep, a * _tile_roll(a, s), a)
        s *= 2
    tiles = []
    for t in range(rows // SUBLANES):
        sl = slice(t * SUBLANES, (t + 1) * SUBLANES)
        h = a[sl] * h0 + b[sl]
        h0 = h[SUBLANES - 1:SUBLANES, :]
        tiles.append(h)
    return jnp.concatenate(tiles, axis=0)


def _chunk_cumsum(x, chunk):
    rows = x.shape[0]
    pos = _tile_pos(x.shape)
    s = 1
    while s < SUBLANES:
        x = jnp.where(pos >= s, x + _tile_roll(x, s), x)
        s *= 2
    tiles = []
    last = None
    for t in range(rows // SUBLANES):
        blk = x[t * SUBLANES:(t + 1) * SUBLANES]
        if t % (chunk // SUBLANES) != 0:
            blk = blk + last
        last = blk[SUBLANES - 1:SUBLANES, :]
        tiles.append(blk)
    return jnp.concatenate(tiles, axis=0)


def _block_row_broadcast(g, block, row):
    pieces = [jnp.broadcast_to(g[b * block + row:b * block + row + 1, :], (block, g.shape[1]))
              for b in range(g.shape[0] // block)]
    return jnp.concatenate(pieces, axis=0)


def _level_weight(g, f, half):
    if half >= 4:
        return jnp.exp2(-jnp.abs(g - _block_row_broadcast(g, 2 * half, half - 1)))
    assert half == 2
    pos = lax.broadcasted_iota(jnp.int32, f.shape, 0) % (2 * half)
    nxt = _tile_roll(f, SUBLANES - 1)
    prv = _tile_roll(f, 1)
    return jnp.where(pos == 0, nxt, jnp.where(pos == 1, 1.0, jnp.where(pos == 2, f, f * prv)))


def _mixer_kernel(layer, steps_per_seq, x_ref, xn_ref, ng_ref, win_f32, rgw_ref, rgb_ref, wri_f32,
                  bri_ref, lam_ref, lbraw_ref, scw_ref, og_ref, ones_ref, wout_f32, o_ref,
                  win_ref, wri_ref, wout_ref, z_first, z_second, ext_a, ext_c, h_carry, st_ref,
                  o_first, o_second):
    step = pl.program_id(0)
    tb = xn_ref.shape[0]
    first, second = slice(0, tb), slice(tb, 2 * tb)

    @pl.when(step == 0)
    def _():
        for src, dst in ((win_f32, win_ref), (wri_f32, wri_ref), (wout_f32, wout_ref)):
            for r0 in range(0, src.shape[0], WEIGHT_CAST_ROWS):
                rs = slice(r0, min(r0 + WEIGHT_CAST_ROWS, src.shape[0]))
                dst[rs, :] = src[rs, :].astype(BF16)

    def projection_tiles(src_ref, src_rows, dst_ref):
        hn = _rmsnorm(src_ref[src_rows, :], ng_ref[...]).astype(BF16)

        def tile(j):
            cols = slice(j * PROJ_TILE, (j + 1) * PROJ_TILE)
            dst_ref[:, cols] = jnp.dot(hn, win_ref[:, cols], preferred_element_type=F32)

        return [functools.partial(tile, j) for j in range(win_ref.shape[1] // PROJ_TILE)]

    @pl.when(step == 0)
    def _():
        for tile in projection_tiles(x_ref, first, z_first):
            tile()

    @pl.when(step % steps_per_seq == 0)
    def _():
        ext_a[0:HALO, :] = jnp.zeros((HALO, RG_WIDTH), F32)
        ext_c[0:HALO, :] = jnp.zeros((HALO, SC_WIDTH), F32)
        h_carry[...] = jnp.zeros(h_carry.shape, F32)
        st_ref[...] = jnp.zeros(st_ref.shape, F32)

    mix = functools.partial(
        _mixer_step, layer, x_ref, rgw_ref=rgw_ref, rgb_ref=rgb_ref, wri_ref=wri_ref,
        bri_ref=bri_ref, lam_ref=lam_ref, lbraw_ref=lbraw_ref, scw_ref=scw_ref, og_ref=og_ref,
        ones_ref=ones_ref, wout_ref=wout_ref, o_ref=o_ref, ext_a=ext_a, ext_c=ext_c,
        h_carry=h_carry, st_ref=st_ref)
    blk_a = mix(first, z_first, projection_tiles(x_ref, second, z_second), o_first)
    blk_b = mix(second, z_second, projection_tiles(xn_ref, slice(None), z_first), o_second)
    next(blk_a)
    next(blk_a)
    next(blk_b)
    for _ in blk_a:
        pass
    for _ in blk_b:
        pass


def _mixer_step(layer, x_ref, block_rows, z_ref, side_work, o_s, *, rgw_ref, rgb_ref, wri_ref,
                bri_ref, lam_ref, lbraw_ref, scw_ref, og_ref, ones_ref, wout_ref, o_ref, ext_a,
                ext_c, h_carry, st_ref):
    side_work = list(side_work)

    def interleave(n=1):
        for _ in range(n):
            if side_work:
                side_work.pop(0)()

    rows = z_ref.shape[0]

    def zcol(i, row_slice=slice(None)):
        return z_ref[row_slice, PROJ_OFFSETS[i]:PROJ_OFFSETS[i + 1]]

    ones_bd = ones_ref[...]
    gains = og_ref[...]

    u = _causal_conv(ext_a, zcol(0), rgw_ref, RG_CONV, rows) + rgb_ref[...]
    ri = _dot(u, wri_ref[...]) + bri_ref[...]
    interleave(2)
    decay_rate = -RG_C * jax.nn.softplus(-lam_ref[...])
    decay_rate2 = decay_rate * LOG2_E
    h_last = h_carry[0:1, :]
    gated = []
    for t0 in range(0, rows, RG_ROW_CHUNK):
        sl = slice(t0, t0 + RG_ROW_CHUNK)
        r = jax.nn.sigmoid(ri[sl, :RG_WIDTH])
        ig = jax.nn.sigmoid(ri[sl, RG_WIDTH:])
        th = jnp.tanh(r * decay_rate)
        beta = jnp.sqrt(-2.0 * th / (1.0 - th))
        h = _linear_scan(jnp.exp2(r * decay_rate2), beta * ig * u[sl], h_last)
        h_last = h[RG_ROW_CHUNK - 1:RG_ROW_CHUNK, :]
        gated.append(h * jax.nn.gelu(zcol(1, sl)))
        if t0 == rows // 2:
            interleave(2)
    h_carry[...] = jnp.broadcast_to(h_last, h_carry.shape)
    y_a = _head_rmsnorm(jnp.concatenate(gated, axis=0), ones_bd) * gains[:, 0:RG_WIDTH]
    interleave(1)

    conv = _causal_conv(ext_c, zcol(7) * zcol(8), scw_ref, SC_CONV, rows)
    y_c = (_head_rmsnorm(zcol(6) * conv, ones_bd[:SC_WIDTH, :SC_WIDTH])
           * gains[:, RG_WIDTH + HG_WIDTH:])
    yield

    raw = lbraw_ref[...]
    e = jnp.exp(raw - jnp.max(raw, axis=0, keepdims=True))
    sm = e / jnp.sum(e, axis=0, keepdims=True)
    lb = sm[0:1, :] - sm[0:1, :]
    for j in range(1, layer + 1):
        lb = lb + sm[j:j + 1, :]

    def chunk_operands(c):
        rs = slice(c * HG_CHUNK, (c + 1) * HG_CHUNK)
        f = lb + (1.0 - lb) * jax.nn.sigmoid(zcol(3, rs))
        qh = jax.nn.silu(zcol(2, rs)) * (HEAD_DIM ** -0.5)
        kk = 1.0 - f
        g = _chunk_cumsum(jnp.log2(f), HG_CHUNK)
        g_last = g[HG_CHUNK - 1:HG_CHUNK, :]
        ops = dict(decay=jnp.exp2(g_last),
                   q_inter=(qh * jnp.exp2(g)).astype(BF16),
                   k_state=(kk * jnp.exp2(g_last - g)).astype(BF16),
                   v=zcol(4, rs).astype(BF16))
        ops["band_e"] = [(qh * kk).astype(BF16), (qh * f * _tile_roll(kk, 1)).astype(BF16)]
        ops["level_q"], ops["level_k"] = [], []
        for hf in MATMUL_LEVEL_HALVES:
            w = _level_weight(g, f, hf)
            ops["level_q"].append((qh * w).astype(BF16))
            ops["level_k"].append((kk * w).astype(BF16))
        return ops

    ti = lax.broadcasted_iota(jnp.int32, (2 * HG_CHUNK, HEAD_DIM), 0) % HG_CHUNK
    si = lax.broadcasted_iota(jnp.int32, (2 * HG_CHUNK, HEAD_DIM), 1)
    top = ti ^ si
    band_masks = [ti == si, (top == 1) & (ti > si)]
    level_masks = [(top >= hf) & (top < 2 * hf) & (ti > si) for hf in MATMUL_LEVEL_HALVES]
    lane = lax.broadcasted_iota(jnp.int32, (HG_CHUNK, LANES), 1)
    head0 = lane < HEAD_DIM
    sr = lax.broadcasted_iota(jnp.int32, (LANES, LANES), 0) // HEAD_DIM
    sc = lax.broadcasted_iota(jnp.int32, (LANES, LANES), 1) // HEAD_DIM
    same_head = sr == sc
    head_ones = jnp.ones((LANES, HEAD_DIM), BF16)

    def head_stack(m):
        zero = jnp.zeros_like(m)
        return jnp.concatenate([jnp.where(head0, m, zero), jnp.where(head0, zero, m)], axis=0)

    pairs = HG_WIDTH // LANES
    states = [st_ref[p] for p in range(pairs)]
    def chunk_scores(ops, p):
        ls = slice(p * LANES, (p + 1) * LANES)
        scores = jnp.zeros((2 * HG_CHUNK, HEAD_DIM), F32)
        for mask, be in zip(band_masks, ops["band_e"]):
            scores = jnp.where(mask, _dot(head_stack(be[:, ls]), head_ones), scores)
        for mask, lq, lk in zip(level_masks, ops["level_q"], ops["level_k"]):
            scores = jnp.where(mask, _dot_nt(head_stack(lq[:, ls]), lk[:, ls]), scores)
        return scores.astype(BF16)

    def chunk_output(c, ops, p, scores):
        rs = slice(c * HG_CHUNK, (c + 1) * HG_CHUNK)
        ls = slice(p * LANES, (p + 1) * LANES)
        v_p = ops["v"][:, ls]
        o_inter = _dot_nt(ops["q_inter"][:, ls], states[p])
        states[p] = (states[p] * ops["decay"][:, ls]
                     + jnp.where(same_head, _dot_tn(v_p, ops["k_state"][:, ls]), 0.0))
        o_stack = _dot(scores, v_p)
        o_s[rs, ls] = o_inter + jnp.where(head0, o_stack[:HG_CHUNK], o_stack[HG_CHUNK:])

    pending = None
    for c in range(rows // HG_CHUNK):
        ops = chunk_operands(c)
        for p in range(pairs):
            scores = chunk_scores(ops, p)
            if pending is not None:
                chunk_output(*pending)
            pending = (c, ops, p, scores)
            interleave()
    chunk_output(*pending)
    for p in range(pairs):
        st_ref[p] = states[p]
    out_gate = gains[:, RG_WIDTH:RG_WIDTH + HG_WIDTH] * jax.nn.silu(zcol(5))
    interleave(len(side_work))
    yield

    y_b = _head_rmsnorm(o_s[...], ones_bd) * out_gate
    y = jnp.concatenate([y_a, y_b, y_c], axis=-1)
    o_ref[block_rows, :] = x_ref[block_rows, :] + _dot(y, wout_ref[...])


def _ffn_kernel(last, x_ref, p_ref, ng_ref, wup_ref, cw_ref, wdown_ref, wgate_ref, wproj_ref,
                fg_ref, o_ref, ext):
    rows = x_ref.shape[0]
    d_ff = wdown_ref.shape[0]

    @pl.when(pl.program_id(1) == 0)
    def _():
        ext[0:HALO, :] = jnp.zeros((HALO, ext.shape[1]), F32)

    x = x_ref[...]
    gu = _causal_conv(ext, _dot(_rmsnorm(x, ng_ref[...]), wup_ref[...]), cw_ref, FFN_CONV, rows)
    act = jax.nn.silu(gu[:, :d_ff]) * gu[:, d_ff:]
    x = x + _dot(act, wdown_ref[...])
    gate = jax.nn.sigmoid(_dot(x, wgate_ref[...]))
    x = x + gate * _dot(p_ref[...], wproj_ref[...])
    if last:
        x = _rmsnorm(x, fg_ref[...])
    o_ref[...] = x


def _const_spec(shape, grid_rank=2):
    zeros = (0,) * len(shape)
    index_map = (lambda s: zeros) if grid_rank == 1 else (lambda b, t: zeros)
    return pl.BlockSpec(shape, index_map, pipeline_mode=pl.Buffered(1))


def _layer_spec(stacked, layer, grid_rank=2):
    shape = stacked.shape[1:]
    index = (layer,) + (0,) * len(shape)
    index_map = (lambda s: index) if grid_rank == 1 else (lambda b, t: index)
    return pl.BlockSpec((None,) + shape, index_map, pipeline_mode=pl.Buffered(1))


def _block_diag(w):
    n, d, e = w.shape
    eye = jnp.eye(n, dtype=w.dtype)
    return jnp.einsum('nde,nm->ndme', w, eye).reshape(n * d, n * e)


def _mixer_call(layer, x, ng, win, rgw, rgb, wri, bri, lam, lbraw, scw, og, ones_bd, wout):
    batch, seq, d_model = x.shape
    tb = min(TIME_BLOCK, seq)
    assert seq % tb == 0 and tb % HG_CHUNK == 0
    assert seq % (2 * tb) == 0
    nt = seq // (2 * tb)
    steps = batch * nt
    row_spec = pl.BlockSpec((None, 2 * tb, d_model), lambda s: (s // nt, s % nt, 0))

    def next_block(s):
        nxt = jnp.minimum(s + 1, steps - 1)
        return (nxt // nt, 2 * (nxt % nt), 0)

    next_spec = pl.BlockSpec((None, tb, d_model), next_block)
    consts = (ng, win, rgw, rgb, wri, bri, lam, lbraw, scw, og, ones_bd, wout)
    specs = [_layer_spec(c, layer, 1) if c is win or c is wout else _const_spec(c.shape, 1)
             for c in consts]
    return pl.pallas_call(
        functools.partial(_mixer_kernel, layer, nt),
        grid=(steps,),
        in_specs=[row_spec, next_spec] + specs,
        out_specs=row_spec,
        out_shape=jax.ShapeDtypeStruct(x.shape, x.dtype),
        scratch_shapes=[
            pltpu.VMEM(win.shape[1:], BF16),
            pltpu.VMEM(wri.shape, BF16),
            pltpu.VMEM(wout.shape[1:], BF16),
            pltpu.VMEM((tb, win.shape[2]), F32),
            pltpu.VMEM((tb, win.shape[2]), F32),
            pltpu.VMEM((tb + HALO, RG_WIDTH), F32),
            pltpu.VMEM((tb + HALO, SC_WIDTH), F32),
            pltpu.VMEM((HALO, RG_WIDTH), F32),
            pltpu.VMEM((HG_WIDTH // LANES, LANES, LANES), F32),
            pltpu.VMEM((tb, HG_WIDTH), F32),
            pltpu.VMEM((tb, HG_WIDTH), F32),
        ],
        compiler_params=pltpu.CompilerParams(
            dimension_semantics=("arbitrary",), vmem_limit_bytes=VMEM_LIMIT_BYTES),
        name=f"mixer_l{layer}",
    )(x, x, *consts)


def _ffn_call(layer, last, x, p, ng, wup, cw, wdown, wgate, wproj, fg):
    batch, seq, d_model = x.shape
    tb = min(FFN_TIME_BLOCK, seq)
    assert seq % tb == 0
    row_spec = pl.BlockSpec((None, tb, d_model), lambda b, t: (b, t, 0))
    p_spec = pl.BlockSpec((None, None, tb, p.shape[-1]), lambda b, t: (layer, b, t, 0))
    consts = (ng, wup, cw, wdown, wgate, wproj, fg)
    stacked = (wup, wdown, wgate, wproj)
    specs = [_layer_spec(c, layer) if any(c is s for s in stacked) else _const_spec(c.shape)
             for c in consts]
    return pl.pallas_call(
        functools.partial(_ffn_kernel, last),
        grid=(batch, seq // tb),
        in_specs=[row_spec, p_spec] + specs,
        out_specs=row_spec,
        out_shape=jax.ShapeDtypeStruct(x.shape, x.dtype),
        scratch_shapes=[pltpu.VMEM((tb + HALO, wup.shape[2]), F32)],
        compiler_params=pltpu.CompilerParams(
            dimension_semantics=("arbitrary", "arbitrary"), vmem_limit_bytes=VMEM_LIMIT_BYTES),
        name=f"ffn_l{layer}",
    )(x, p, *consts)


def kernel(x, p, norm_mix_gain, w_in, rg_conv_w, rg_conv_b, rg_w_r, rg_b_r, rg_w_i, rg_b_i, rg_lambda, hg_lower_bounds, sc_conv_w, mix_out_gain, w_out, norm_ffn_gain, ffn_w_up, ffn_conv_w, ffn_w_down, ple_w_proj, ple_w_gate, final_norm_gain):
    depth = w_in.shape[0]
    head_id = jnp.arange(RG_WIDTH) // HEAD_DIM
    ones_bd = (head_id[:, None] == head_id[None, :]).astype(BF16)
    row = lambda v: v.reshape(1, -1)
    ffn_weights = [w.astype(BF16) for w in (ffn_w_up, ffn_w_down, ple_w_gate, ple_w_proj)]
    for i in range(depth):
        wri = jnp.concatenate([_block_diag(rg_w_r[i]), _block_diag(rg_w_i[i])], axis=1)
        bri = jnp.concatenate([rg_b_r[i].reshape(-1), rg_b_i[i].reshape(-1)]).reshape(1, -1)
        x = _mixer_call(i, x, row(norm_mix_gain[i]), w_in, rg_conv_w[i],
                        row(rg_conv_b[i]), wri, bri, row(rg_lambda[i]), hg_lower_bounds,
                        sc_conv_w[i], row(mix_out_gain[i]), ones_bd, w_out)
        wup, wdown, wgate, wproj = ffn_weights
        x = _ffn_call(i, i == depth - 1, x, p, row(norm_ffn_gain[i]), wup, ffn_conv_w[i], wdown,
                      wgate, wproj, row(final_norm_gain))
    return x
```

```python
import functools

import jax
import jax.numpy as jnp
import numpy as np
from jax import lax
from jax.experimental import pallas as pl
from jax.experimental.pallas import tpu as pltpu

HEAD_DIM = 64
RG_WIDTH = 384
HG_WIDTH = 384
SC_WIDTH = 256
RG_CONV = 4
SC_CONV = 3
FFN_CONV = 3
RG_C = 8.0
HG_CHUNK = 64
EPS = 1e-6
LOG2_E = 1.4426950408889634
SUBLANES = 8
LANES = 128
HALO = SUBLANES
TIME_BLOCK = 256
FFN_TIME_BLOCK = 512
PROJ_TILE = 256
WEIGHT_CAST_ROWS = 128
RG_ROW_CHUNK = 16
VMEM_LIMIT_BYTES = 56 * 1024 * 1024
MATMUL_LEVEL_HALVES = (32, 16, 8, 4, 2)

_sizes = (RG_WIDTH, RG_WIDTH, HG_WIDTH, HG_WIDTH, HG_WIDTH, HG_WIDTH, SC_WIDTH, SC_WIDTH, SC_WIDTH)
PROJ_OFFSETS = tuple(sum(_sizes[:i]) for i in range(len(_sizes) + 1))

F32 = jnp.float32
BF16 = jnp.bfloat16


def _dot(a, b):
    return jnp.dot(a.astype(BF16), b.astype(BF16), preferred_element_type=F32)


def _dot_nt(a, b):
    return lax.dot_general(a.astype(BF16), b.astype(BF16), (((1,), (1,)), ((), ())),
                           preferred_element_type=F32)


def _dot_tn(a, b):
    return lax.dot_general(a.astype(BF16), b.astype(BF16), (((0,), (0,)), ((), ())),
                           preferred_element_type=F32)


def _rmsnorm(x, gain):
    ms = jnp.mean(x * x, axis=-1, keepdims=True)
    return x * lax.rsqrt(ms + EPS) * gain


def _head_rmsnorm(y, ones_bd):
    ssum = _dot(y * y, ones_bd)
    return y * lax.rsqrt(ssum * (1.0 / HEAD_DIM) + EPS)


def _causal_conv(ext_ref, new_rows, w_ref, taps, rows):
    ext_ref[HALO:HALO + rows, :] = new_rows
    out = None
    for k in range(taps):
        term = w_ref[k:k + 1, :] * ext_ref[pl.ds(HALO - (taps - 1) + k, rows), :]
        out = term if out is None else out + term
    ext_ref[0:HALO, :] = ext_ref[rows:rows + HALO, :]
    return out


def _tile_roll(x, shift):
    rows, w = x.shape
    x3 = x.reshape(rows // SUBLANES, SUBLANES, w)
    return pltpu.roll(x3, shift, axis=1).reshape(rows, w)


def _tile_pos(shape):
    return lax.broadcasted_iota(jnp.int32, shape, 0) % SUBLANES


def _linear_scan(a, b, h0):
    rows = a.shape[0]
    pos = _tile_pos(a.shape)
    s = 1
    while s < SUBLANES:
        keep = pos >= s
        b = jnp.where(keep, a * _tile_roll(b, s) + b, b)
        a = jnp.where(keep, a * _tile_roll(a, s), a)
        s *= 2
    tiles = []
    for t in range(rows // SUBLANES):
        sl = slice(t * SUBLANES, (t + 1) * SUBLANES)
        h = a[sl] * h0 + b[sl]
        h0 = h[SUBLANES - 1:SUBLANES, :]
        tiles.append(h)
    return jnp.concatenate(tiles, axis=0)


def _chunk_cumsum(x, chunk):
    rows = x.shape[0]
    pos = _tile_pos(x.shape)
    s = 1
    while s < SUBLANES:
        x = jnp.where(pos >= s, x + _tile_roll(x, s), x)
        s *= 2
    tiles = []
    last = None
    for t in range(rows // SUBLANES):
        blk = x[t * SUBLANES:(t + 1) * SUBLANES]
        if t % (chunk // SUBLANES) != 0:
            blk = blk + last
        last = blk[SUBLANES - 1:SUBLANES, :]
        tiles.append(blk)
    return jnp.concatenate(tiles, axis=0)


def _block_row_broadcast(g, block, row):
    pieces = [jnp.broadcast_to(g[b * block + row:b * block + row + 1, :], (block, g.shape[1]))
              for b in range(g.shape[0] // block)]
    return jnp.concatenate(pieces, axis=0)


def _level_weight(g, f, half):
    if half >= 4:
        return jnp.exp2(-jnp.abs(g - _block_row_broadcast(g, 2 * half, half - 1)))
    assert half == 2
    pos = lax.broadcasted_iota(jnp.int32, f.shape, 0) % (2 * half)
    nxt = _tile_roll(f, SUBLANES - 1)
    prv = _tile_roll(f, 1)
    return jnp.where(pos == 0, nxt, jnp.where(pos == 1, 1.0, jnp.where(pos == 2, f, f * prv)))


def _mixer_kernel(layer, steps_per_seq, x_ref, xn_ref, ng_ref, win_f32, rgw_ref, rgb_ref, wri_f32,
                  bri_ref, lam_ref, lbraw_ref, scw_ref, og_ref, ones_ref, wout_f32, o_ref,
                  win_ref, wri_ref, wout_ref, z_first, z_second, ext_a, ext_c, h_carry, st_ref,
                  o_first, o_second):
    step = pl.program_id(0)
    tb = xn_ref.shape[0]
    first, second = slice(0, tb), slice(tb, 2 * tb)

    @pl.when(step == 0)
    def _():
        for src, dst in ((win_f32, win_ref), (wri_f32, wri_ref), (wout_f32, wout_ref)):
            for r0 in range(0, src.shape[0], WEIGHT_CAST_ROWS):
                rs = slice(r0, min(r0 + WEIGHT_CAST_ROWS, src.shape[0]))
                dst[rs, :] = src[rs, :].astype(BF16)

    def projection_tiles(src_ref, src_rows, dst_ref):
        hn = _rmsnorm(src_ref[src_rows, :], ng_ref[...]).astype(BF16)

        def tile(j):
            cols = slice(j * PROJ_TILE, (j + 1) * PROJ_TILE)
            dst_ref[:, cols] = jnp.dot(hn, win_ref[:, cols], preferred_element_type=F32)

        return [functools.partial(tile, j) for j in range(win_ref.shape[1] // PROJ_TILE)]

    @pl.when(step == 0)
    def _():
        for tile in projection_tiles(x_ref, first, z_first):
            tile()

    @pl.when(step % steps_per_seq == 0)
    def _():
        ext_a[0:HALO, :] = jnp.zeros((HALO, RG_WIDTH), F32)
        ext_c[0:HALO, :] = jnp.zeros((HALO, SC_WIDTH), F32)
        h_carry[...] = jnp.zeros(h_carry.shape, F32)
        st_ref[...] = jnp.zeros(st_ref.shape, F32)

    mix = functools.partial(
        _mixer_step, layer, x_ref, rgw_ref=rgw_ref, rgb_ref=rgb_ref, wri_ref=wri_ref,
        bri_ref=bri_ref, lam_ref=lam_ref, lbraw_ref=lbraw_ref, scw_ref=scw_ref, og_ref=og_ref,
        ones_ref=ones_ref, wout_ref=wout_ref, o_ref=o_ref, ext_a=ext_a, ext_c=ext_c,
        h_carry=h_carry, st_ref=st_ref)
    blk_a = mix(first, z_first, projection_tiles(x_ref, second, z_second), o_first)
    blk_b = mix(second, z_second, projection_tiles(xn_ref, slice(None), z_first), o_second)
    next(blk_a)
    next(blk_a)
    next(blk_b)
    for _ in blk_a:
        pass
    for _ in blk_b:
        pass


def _mixer_step(layer, x_ref, block_rows, z_ref, side_work, o_s, *, rgw_ref, rgb_ref, wri_ref,
                bri_ref, lam_ref, lbraw_ref, scw_ref, og_ref, ones_ref, wout_ref, o_ref, ext_a,
                ext_c, h_carry, st_ref):
    side_work = list(side_work)

    def interleave(n=1):
        for _ in range(n):
            if side_work:
                side_work.pop(0)()

    rows = z_ref.shape[0]

    def zcol(i, row_slice=slice(None)):
        return z_ref[row_slice, PROJ_OFFSETS[i]:PROJ_OFFSETS[i + 1]]

    ones_bd = ones_ref[...]
    gains = og_ref[...]

    u = _causal_conv(ext_a, zcol(0), rgw_ref, RG_CONV, rows) + rgb_ref[...]
    ri = _dot(u, wri_ref[...]) + bri_ref[...]
    interleave(2)
    decay_rate = -RG_C * jax.nn.softplus(-lam_ref[...])
    decay_rate2 = decay_rate * LOG2_E
    h_last = h_carry[0:1, :]
    gated = []
    for t0 in range(0, rows, RG_ROW_CHUNK):
        sl = slice(t0, t0 + RG_ROW_CHUNK)
        r = jax.nn.sigmoid(ri[sl, :RG_WIDTH])
        ig = jax.nn.sigmoid(ri[sl, RG_WIDTH:])
        th = jnp.tanh(r * decay_rate)
        beta = jnp.sqrt(-2.0 * th / (1.0 - th))
        h = _linear_scan(jnp.exp2(r * decay_rate2), beta * ig * u[sl], h_last)
        h_last = h[RG_ROW_CHUNK - 1:RG_ROW_CHUNK, :]
        gated.append(h * jax.nn.gelu(zcol(1, sl)))
        if t0 == rows // 2:
            interleave(2)
    h_carry[...] = jnp.broadcast_to(h_last, h_carry.shape)
    y_a = _head_rmsnorm(jnp.concatenate(gated, axis=0), ones_bd) * gains[:, 0:RG_WIDTH]
    interleave(1)

    conv = _causal_conv(ext_c, zcol(7) * zcol(8), scw_ref, SC_CONV, rows)
    y_c = (_head_rmsnorm(zcol(6) * conv, ones_bd[:SC_WIDTH, :SC_WIDTH])
           * gains[:, RG_WIDTH + HG_WIDTH:])
    yield

    raw = lbraw_ref[...]
    e = jnp.exp(raw - jnp.max(raw, axis=0, keepdims=True))
    sm = e / jnp.sum(e, axis=0, keepdims=True)
    lb = sm[0:1, :] - sm[0:1, :]
    for j in range(1, layer + 1):
        lb = lb + sm[j:j + 1, :]

    def chunk_operands(c):
        rs = slice(c * HG_CHUNK, (c + 1) * HG_CHUNK)
        f = lb + (1.0 - lb) * jax.nn.sigmoid(zcol(3, rs))
        qh = jax.nn.silu(zcol(2, rs)) * (HEAD_DIM ** -0.5)
        kk = 1.0 - f
        g = _chunk_cumsum(jnp.log2(f), HG_CHUNK)
        g_last = g[HG_CHUNK - 1:HG_CHUNK, :]
        ops = dict(decay=jnp.exp2(g_last),
                   q_inter=(qh * jnp.exp2(g)).astype(BF16),
                   k_state=(kk * jnp.exp2(g_last - g)).astype(BF16),
                   v=zcol(4, rs).astype(BF16))
        ops["band_e"] = [(qh * kk).astype(BF16), (qh * f * _tile_roll(kk, 1)).astype(BF16)]
        ops["level_q"], ops["level_k"] = [], []
        for hf in MATMUL_LEVEL_HALVES:
            w = _level_weight(g, f, hf)
            ops["level_q"].append((qh * w).astype(BF16))
            ops["level_k"].append((kk * w).astype(BF16))
        return ops

    ti = lax.broadcasted_iota(jnp.int32, (2 * HG_CHUNK, HEAD_DIM), 0) % HG_CHUNK
    si = lax.broadcasted_iota(jnp.int32, (2 * HG_CHUNK, HEAD_DIM), 1)
    top = ti ^ si
    band_masks = [ti == si, (top == 1) & (ti > si)]
    level_masks = [(top >= hf) & (top < 2 * hf) & (ti > si) for hf in MATMUL_LEVEL_HALVES]
    lane = lax.broadcasted_iota(jnp.int32, (HG_CHUNK, LANES), 1)
    head0 = lane < HEAD_DIM
    sr = lax.broadcasted_iota(jnp.int32, (LANES, LANES), 0) // HEAD_DIM
    sc = lax.broadcasted_iota(jnp.int32, (LANES, LANES), 1) // HEAD_DIM
    same_head = sr == sc
    head_ones = jnp.ones((LANES, HEAD_DIM), BF16)

    def head_stack(m):
        zero = jnp.zeros_like(m)
        return jnp.concatenate([jnp.where(head0, m, zero), jnp.where(head0, zero, m)], axis=0)

    pairs = HG_WIDTH // LANES
    states = [st_ref[p] for p in range(pairs)]
    def chunk_scores(ops, p):
        ls = slice(p * LANES, (p + 1) * LANES)
        scores = jnp.zeros((2 * HG_CHUNK, HEAD_DIM), F32)
        for mask, be in zip(band_masks, ops["band_e"]):
            scores = jnp.where(mask, _dot(head_stack(be[:, ls]), head_ones), scores)
        for mask, lq, lk in zip(level_masks, ops["level_q"], ops["level_k"]):
            scores = jnp.where(mask, _dot_nt(head_stack(lq[:, ls]), lk[:, ls]), scores)
        return scores.astype(BF16)

    def chunk_output(c, ops, p, scores):
        rs = slice(c * HG_CHUNK, (c + 1) * HG_CHUNK)
        ls = slice(p * LANES, (p + 1) * LANES)
        v_p = ops["v"][:, ls]
        o_inter = _dot_nt(ops["q_inter"][:, ls], states[p])
        states[p] = (states[p] * ops["decay"][:, ls]
                     + jnp.where(same_head, _dot_tn(v_p, ops["k_state"][:, ls]), 0.0))
        o_stack = _dot(scores, v_p)
        o_s[rs, ls] = o_inter + jnp.where(head0, o_stack[:HG_CHUNK], o_stack[HG_CHUNK:])

    pending = None
    for c in range(rows // HG_CHUNK):
        ops = chunk_operands(c)
        for p in range(pairs):
            scores = chunk_scores(ops, p)
            if pending is not None:
                chunk_output(*pending)
            pending = (c, ops, p, scores)
            interleave()
    chunk_output(*pending)
    for p in range(pairs):
        st_ref[p] = states[p]
    out_gate = gains[:, RG_WIDTH:RG_WIDTH + HG_WIDTH] * jax.nn.silu(zcol(5))
    interleave(len(side_work))
    yield

    y_b = _head_rmsnorm(o_s[...], ones_bd) * out_gate
    y = jnp.concatenate([y_a, y_b, y_c], axis=-1)
    o_ref[block_rows, :] = x_ref[block_rows, :] + _dot(y, wout_ref[...])


def _ffn_kernel(last, x_ref, p_ref, ng_ref, wup_ref, cw_ref, wdown_ref, wgate_ref, wproj_ref,
                fg_ref, o_ref, ext):
    rows = x_ref.shape[0]
    d_ff = wdown_ref.shape[0]

    @pl.when(pl.program_id(1) == 0)
    def _():
        ext[0:HALO, :] = jnp.zeros((HALO, ext.shape[1]), F32)

    x = x_ref[...]
    gu = _causal_conv(ext, _dot(_rmsnorm(x, ng_ref[...]), wup_ref[...]), cw_ref, FFN_CONV, rows)
    act = jax.nn.silu(gu[:, :d_ff]) * gu[:, d_ff:]
    x = x + _dot(act, wdown_ref[...])
    gate = jax.nn.sigmoid(_dot(x, wgate_ref[...]))
    x = x + gate * _dot(p_ref[...], wproj_ref[...])
    if last:
        x = _rmsnorm(x, fg_ref[...])
    o_ref[...] = x


def _const_spec(shape, grid_rank=2):
    zeros = (0,) * len(shape)
    index_map = (lambda s: zeros) if grid_rank == 1 else (lambda b, t: zeros)
    return pl.BlockSpec(shape, index_map, pipeline_mode=pl.Buffered(1))


def _layer_spec(stacked, layer, grid_rank=2):
    shape = stacked.shape[1:]
    index = (layer,) + (0,) * len(shape)
    index_map = (lambda s: index) if grid_rank == 1 else (lambda b, t: index)
    return pl.BlockSpec((None,) + shape, index_map, pipeline_mode=pl.Buffered(1))


def _block_diag(w):
    layers, n, d, e = w.shape
    eye = jnp.eye(n, dtype=w.dtype)
    return jnp.einsum('lnde,nm->lndme', w, eye).reshape(layers, n * d, n * e)


def _mixer_call(layer, x, ng, win, rgw, rgb, wri, bri, lam, lbraw, scw, og, ones_bd, wout):
    batch, seq, d_model = x.shape
    tb = min(TIME_BLOCK, seq)
    assert seq % tb == 0 and tb % HG_CHUNK == 0
    assert seq % (2 * tb) == 0
    nt = seq // (2 * tb)
    steps = batch * nt
    row_spec = pl.BlockSpec((None, 2 * tb, d_model), lambda s: (s // nt, s % nt, 0))

    def next_block(s):
        nxt = jnp.minimum(s + 1, steps - 1)
        return (nxt // nt, 2 * (nxt % nt), 0)

    next_spec = pl.BlockSpec((None, tb, d_model), next_block)
    consts = (ng, win, rgw, rgb, wri, bri, lam, lbraw, scw, og, ones_bd, wout)
    specs = [_const_spec(c.shape, 1) if c is lbraw or c is ones_bd else _layer_spec(c, layer, 1)
             for c in consts]
    return pl.pallas_call(
        functools.partial(_mixer_kernel, layer, nt),
        grid=(steps,),
        in_specs=[row_spec, next_spec] + specs,
        out_specs=row_spec,
        out_shape=jax.ShapeDtypeStruct(x.shape, x.dtype),
        scratch_shapes=[
            pltpu.VMEM(win.shape[1:], BF16),
            pltpu.VMEM(wri.shape[1:], BF16),
            pltpu.VMEM(wout.shape[1:], BF16),
            pltpu.VMEM((tb, win.shape[2]), F32),
            pltpu.VMEM((tb, win.shape[2]), F32),
            pltpu.VMEM((tb + HALO, RG_WIDTH), F32),
            pltpu.VMEM((tb + HALO, SC_WIDTH), F32),
            pltpu.VMEM((HALO, RG_WIDTH), F32),
            pltpu.VMEM((HG_WIDTH // LANES, LANES, LANES), F32),
            pltpu.VMEM((tb, HG_WIDTH), F32),
            pltpu.VMEM((tb, HG_WIDTH), F32),
        ],
        compiler_params=pltpu.CompilerParams(
            dimension_semantics=("arbitrary",), vmem_limit_bytes=VMEM_LIMIT_BYTES),
        name=f"mixer_l{layer}",
    )(x, x, *consts)


def _ffn_call(layer, last, x, p, ng, wup, cw, wdown, wgate, wproj, fg):
    batch, seq, d_model = x.shape
    tb = min(FFN_TIME_BLOCK, seq)
    assert seq % tb == 0
    row_spec = pl.BlockSpec((None, tb, d_model), lambda b, t: (b, t, 0))
    p_spec = pl.BlockSpec((None, None, tb, p.shape[-1]), lambda b, t: (layer, b, t, 0))
    consts = (ng, wup, cw, wdown, wgate, wproj, fg)
    specs = [_const_spec(c.shape) if c is fg else _layer_spec(c, layer) for c in consts]
    return pl.pallas_call(
        functools.partial(_ffn_kernel, last),
        grid=(batch, seq // tb),
        in_specs=[row_spec, p_spec] + specs,
        out_specs=row_spec,
        out_shape=jax.ShapeDtypeStruct(x.shape, x.dtype),
        scratch_shapes=[pltpu.VMEM((tb + HALO, wup.shape[2]), F32)],
        compiler_params=pltpu.CompilerParams(
            dimension_semantics=("arbitrary", "arbitrary"), vmem_limit_bytes=VMEM_LIMIT_BYTES),
        name=f"ffn_l{layer}",
    )(x, p, *consts)


def kernel(x, p, norm_mix_gain, w_in, rg_conv_w, rg_conv_b, rg_w_r, rg_b_r, rg_w_i, rg_b_i, rg_lambda, hg_lower_bounds, sc_conv_w, mix_out_gain, w_out, norm_ffn_gain, ffn_w_up, ffn_conv_w, ffn_w_down, ple_w_proj, ple_w_gate, final_norm_gain):
    depth = w_in.shape[0]
    head_id = np.arange(RG_WIDTH) // HEAD_DIM
    ones_bd = jnp.asarray(head_id[:, None] == head_id[None, :], dtype=BF16)
    rows = lambda v: v.reshape(depth, 1, -1)
    wri = jnp.concatenate([_block_diag(rg_w_r), _block_diag(rg_w_i)], axis=-1)
    bri = jnp.concatenate([rg_b_r.reshape(depth, 1, -1), rg_b_i.reshape(depth, 1, -1)], axis=-1)
    wup, wdown, wgate, wproj = (w.astype(BF16)
                                for w in (ffn_w_up, ffn_w_down, ple_w_gate, ple_w_proj))
    for i in range(depth):
        x = _mixer_call(i, x, rows(norm_mix_gain), w_in, rg_conv_w, rows(rg_conv_b), wri, bri,
                        rows(rg_lambda), hg_lower_bounds, sc_conv_w, rows(mix_out_gain), ones_bd,
                        w_out)
        x = _ffn_call(i, i == depth - 1, x, p, rows(norm_ffn_gain), wup, ffn_conv_w, wdown, wgate,
                      wproj, final_norm_gain.reshape(1, -1))
    return x
```

```python
import functools

import jax
import jax.numpy as jnp
import numpy as np
from jax import lax
from jax.experimental import pallas as pl
from jax.experimental.pallas import tpu as pltpu

HEAD_DIM = 64
RG_WIDTH = 384
HG_WIDTH = 384
SC_WIDTH = 256
RG_CONV = 4
SC_CONV = 3
FFN_CONV = 3
RG_C = 8.0
HG_CHUNK = 64
EPS = 1e-6
LOG2_E = 1.4426950408889634
SUBLANES = 8
LANES = 128
HALO = SUBLANES
TIME_BLOCK = 256
FFN_TIME_BLOCK = 512
PROJ_TILE = 256
WEIGHT_CAST_ROWS = 128
WEIGHT_STAGE_ROWS = 256
RG_ROW_CHUNK = 16
VMEM_LIMIT_BYTES = 56 * 1024 * 1024
MATMUL_LEVEL_HALVES = (32, 16, 8, 4, 2)

_sizes = (RG_WIDTH, RG_WIDTH, HG_WIDTH, HG_WIDTH, HG_WIDTH, HG_WIDTH, SC_WIDTH, SC_WIDTH, SC_WIDTH)
PROJ_OFFSETS = tuple(sum(_sizes[:i]) for i in range(len(_sizes) + 1))

F32 = jnp.float32
BF16 = jnp.bfloat16


def _dot(a, b):
    return jnp.dot(a.astype(BF16), b.astype(BF16), preferred_element_type=F32)


def _dot_nt(a, b):
    return lax.dot_general(a.astype(BF16), b.astype(BF16), (((1,), (1,)), ((), ())),
                           preferred_element_type=F32)


def _dot_tn(a, b):
    return lax.dot_general(a.astype(BF16), b.astype(BF16), (((0,), (0,)), ((), ())),
                           preferred_element_type=F32)


def _rmsnorm(x, gain):
    ms = jnp.mean(x * x, axis=-1, keepdims=True)
    return x * lax.rsqrt(ms + EPS) * gain


def _head_rmsnorm(y, ones_bd):
    ssum = _dot(y * y, ones_bd)
    return y * lax.rsqrt(ssum * (1.0 / HEAD_DIM) + EPS)


def _causal_conv(ext_ref, new_rows, w_ref, taps, rows):
    ext_ref[HALO:HALO + rows, :] = new_rows
    out = None
    for k in range(taps):
        term = w_ref[k:k + 1, :] * ext_ref[pl.ds(HALO - (taps - 1) + k, rows), :]
        out = term if out is None else out + term
    ext_ref[0:HALO, :] = ext_ref[rows:rows + HALO, :]
    return out


def _tile_roll(x, shift):
    rows, w = x.shape
    x3 = x.reshape(rows // SUBLANES, SUBLANES, w)
    return pltpu.roll(x3, shift, axis=1).reshape(rows, w)


def _tile_pos(shape):
    return lax.broadcasted_iota(jnp.int32, shape, 0) % SUBLANES


def _linear_scan(a, b, h0):
    rows = a.shape[0]
    pos = _tile_pos(a.shape)
    s = 1
    while s < SUBLANES:
        keep = pos >= s
        b = jnp.where(keep, a * _tile_roll(b, s) + b, b)
        a = jnp.where(keep, a * _tile_roll(a, s), a)
        s *= 2
    tiles = []
    for t in range(rows // SUBLANES):
        sl = slice(t * SUBLANES, (t + 1) * SUBLANES)
        h = a[sl] * h0 + b[sl]
        h0 = h[SUBLANES - 1:SUBLANES, :]
        tiles.append(h)
    return jnp.concatenate(tiles, axis=0)


def _chunk_cumsum(x, chunk):
    rows = x.shape[0]
    pos = _tile_pos(x.shape)
    s = 1
    while s < SUBLANES:
        x = jnp.where(pos >= s, x + _tile_roll(x, s), x)
        s *= 2
    tiles = []
    last = None
    for t in range(rows // SUBLANES):
        blk = x[t * SUBLANES:(t + 1) * SUBLANES]
        if t % (chunk // SUBLANES) != 0:
            blk = blk + last
        last = blk[SUBLANES - 1:SUBLANES, :]
        tiles.append(blk)
    return jnp.concatenate(tiles, axis=0)


def _block_row_broadcast(g, block, row):
    pieces = [jnp.broadcast_to(g[b * block + row:b * block + row + 1, :], (block, g.shape[1]))
              for b in range(g.shape[0] // block)]
    return jnp.concatenate(pieces, axis=0)


def _level_weight(g, f, half):
    if half >= 4:
        return jnp.exp2(-jnp.abs(g - _block_row_broadcast(g, 2 * half, half - 1)))
    assert half == 2
    pos = lax.broadcasted_iota(jnp.int32, f.shape, 0) % (2 * half)
    nxt = _tile_roll(f, SUBLANES - 1)
    prv = _tile_roll(f, 1)
    return jnp.where(pos == 0, nxt, jnp.where(pos == 1, 1.0, jnp.where(pos == 2, f, f * prv)))


def _mixer_kernel(layer, steps_per_seq, x_ref, xn_ref, ng_ref, win_f32, rgw_ref, rgb_ref, wri_f32,
                  bri_ref, lam_ref, lbraw_ref, scw_ref, og_ref, ones_ref, wout_f32, o_ref,
                  win_ref, wri_ref, wout_ref, z_first, z_second, ext_a, ext_c, h_carry, st_ref,
                  o_first, o_second):
    step = pl.program_id(0)
    tb = xn_ref.shape[0]
    first, second = slice(0, tb), slice(tb, 2 * tb)

    @pl.when(step == 0)
    def _():
        for src, dst in ((win_f32, win_ref), (wri_f32, wri_ref), (wout_f32, wout_ref)):
            for r0 in range(0, src.shape[0], WEIGHT_CAST_ROWS):
                rs = slice(r0, min(r0 + WEIGHT_CAST_ROWS, src.shape[0]))
                dst[rs, :] = src[rs, :].astype(BF16)

    def projection_tiles(src_ref, src_rows, dst_ref):
        hn = _rmsnorm(src_ref[src_rows, :], ng_ref[layer:layer + 1, :]).astype(BF16)

        def tile(j):
            cols = slice(j * PROJ_TILE, (j + 1) * PROJ_TILE)
            dst_ref[:, cols] = jnp.dot(hn, win_ref[:, cols], preferred_element_type=F32)

        return [functools.partial(tile, j) for j in range(win_ref.shape[1] // PROJ_TILE)]

    @pl.when(step == 0)
    def _():
        for tile in projection_tiles(x_ref, first, z_first):
            tile()

    @pl.when(step % steps_per_seq == 0)
    def _():
        ext_a[0:HALO, :] = jnp.zeros((HALO, RG_WIDTH), F32)
        ext_c[0:HALO, :] = jnp.zeros((HALO, SC_WIDTH), F32)
        h_carry[...] = jnp.zeros(h_carry.shape, F32)
        st_ref[...] = jnp.zeros(st_ref.shape, F32)

    mix = functools.partial(
        _mixer_step, layer, x_ref, rgw_ref=rgw_ref, rgb_ref=rgb_ref, wri_ref=wri_ref,
        bri_ref=bri_ref, lam_ref=lam_ref, lbraw_ref=lbraw_ref, scw_ref=scw_ref, og_ref=og_ref,
        ones_ref=ones_ref, wout_ref=wout_ref, o_ref=o_ref, ext_a=ext_a, ext_c=ext_c,
        h_carry=h_carry, st_ref=st_ref)
    blk_a = mix(first, z_first, projection_tiles(x_ref, second, z_second), o_first)
    blk_b = mix(second, z_second, projection_tiles(xn_ref, slice(None), z_first), o_second)
    next(blk_a)
    next(blk_a)
    next(blk_b)
    for _ in blk_a:
        pass
    for _ in blk_b:
        pass


def _mixer_step(layer, x_ref, block_rows, z_ref, side_work, o_s, *, rgw_ref, rgb_ref, wri_ref,
                bri_ref, lam_ref, lbraw_ref, scw_ref, og_ref, ones_ref, wout_ref, o_ref, ext_a,
                ext_c, h_carry, st_ref):
    side_work = list(side_work)

    def interleave(n=1):
        for _ in range(n):
            if side_work:
                side_work.pop(0)()

    rows = z_ref.shape[0]

    def zcol(i, row_slice=slice(None)):
        return z_ref[row_slice, PROJ_OFFSETS[i]:PROJ_OFFSETS[i + 1]]

    ones_bd = ones_ref[...]
    layer_row = slice(layer, layer + 1)
    gains = og_ref[layer_row, :]

    u = _causal_conv(ext_a, zcol(0), rgw_ref, RG_CONV, rows) + rgb_ref[layer_row, :]
    ri = _dot(u, wri_ref[...]) + bri_ref[layer_row, :]
    interleave(2)
    decay_rate = -RG_C * jax.nn.softplus(-lam_ref[layer_row, :])
    decay_rate2 = decay_rate * LOG2_E
    h_last = h_carry[0:1, :]
    gated = []
    for t0 in range(0, rows, RG_ROW_CHUNK):
        sl = slice(t0, t0 + RG_ROW_CHUNK)
        r = jax.nn.sigmoid(ri[sl, :RG_WIDTH])
        ig = jax.nn.sigmoid(ri[sl, RG_WIDTH:])
        th = jnp.tanh(r * decay_rate)
        beta = jnp.sqrt(-2.0 * th / (1.0 - th))
        h = _linear_scan(jnp.exp2(r * decay_rate2), beta * ig * u[sl], h_last)
        h_last = h[RG_ROW_CHUNK - 1:RG_ROW_CHUNK, :]
        gated.append(h * jax.nn.gelu(zcol(1, sl)))
        if t0 == rows // 2:
            interleave(2)
    h_carry[...] = jnp.broadcast_to(h_last, h_carry.shape)
    y_a = _head_rmsnorm(jnp.concatenate(gated, axis=0), ones_bd) * gains[:, 0:RG_WIDTH]
    interleave(1)

    conv = _causal_conv(ext_c, zcol(7) * zcol(8), scw_ref, SC_CONV, rows)
    y_c = (_head_rmsnorm(zcol(6) * conv, ones_bd[:SC_WIDTH, :SC_WIDTH])
           * gains[:, RG_WIDTH + HG_WIDTH:])
    yield

    raw = lbraw_ref[...]
    e = jnp.exp(raw - jnp.max(raw, axis=0, keepdims=True))
    sm = e / jnp.sum(e, axis=0, keepdims=True)
    lb = sm[0:1, :] - sm[0:1, :]
    for j in range(1, layer + 1):
        lb = lb + sm[j:j + 1, :]

    def chunk_operands(c):
        rs = slice(c * HG_CHUNK, (c + 1) * HG_CHUNK)
        f = lb + (1.0 - lb) * jax.nn.sigmoid(zcol(3, rs))
        qh = jax.nn.silu(zcol(2, rs)) * (HEAD_DIM ** -0.5)
        kk = 1.0 - f
        g = _chunk_cumsum(jnp.log2(f), HG_CHUNK)
        g_last = g[HG_CHUNK - 1:HG_CHUNK, :]
        ops = dict(decay=jnp.exp2(g_last),
                   q_inter=(qh * jnp.exp2(g)).astype(BF16),
                   k_state=(kk * jnp.exp2(g_last - g)).astype(BF16),
                   v=zcol(4, rs).astype(BF16))
        ops["band_e"] = [(qh * kk).astype(BF16), (qh * f * _tile_roll(kk, 1)).astype(BF16)]
        ops["level_q"], ops["level_k"] = [], []
        for hf in MATMUL_LEVEL_HALVES:
            w = _level_weight(g, f, hf)
            ops["level_q"].append((qh * w).astype(BF16))
            ops["level_k"].append((kk * w).astype(BF16))
        return ops

    ti = lax.broadcasted_iota(jnp.int32, (2 * HG_CHUNK, HEAD_DIM), 0) % HG_CHUNK
    si = lax.broadcasted_iota(jnp.int32, (2 * HG_CHUNK, HEAD_DIM), 1)
    top = ti ^ si
    band_masks = [ti == si, (top == 1) & (ti > si)]
    level_masks = [(top >= hf) & (top < 2 * hf) & (ti > si) for hf in MATMUL_LEVEL_HALVES]
    lane = lax.broadcasted_iota(jnp.int32, (HG_CHUNK, LANES), 1)
    head0 = lane < HEAD_DIM
    sr = lax.broadcasted_iota(jnp.int32, (LANES, LANES), 0) // HEAD_DIM
    sc = lax.broadcasted_iota(jnp.int32, (LANES, LANES), 1) // HEAD_DIM
    same_head = sr == sc
    head_ones = jnp.ones((LANES, HEAD_DIM), BF16)

    def head_stack(m):
        zero = jnp.zeros_like(m)
        return jnp.concatenate([jnp.where(head0, m, zero), jnp.where(head0, zero, m)], axis=0)

    pairs = HG_WIDTH // LANES
    states = [st_ref[p] for p in range(pairs)]
    def chunk_scores(ops, p):
        ls = slice(p * LANES, (p + 1) * LANES)
        scores = jnp.zeros((2 * HG_CHUNK, HEAD_DIM), F32)
        for mask, be in zip(band_masks, ops["band_e"]):
            scores = jnp.where(mask, _dot(head_stack(be[:, ls]), head_ones), scores)
        for mask, lq, lk in zip(level_masks, ops["level_q"], ops["level_k"]):
            scores = jnp.where(mask, _dot_nt(head_stack(lq[:, ls]), lk[:, ls]), scores)
        return scores.astype(BF16)

    def chunk_output(c, ops, p, scores):
        rs = slice(c * HG_CHUNK, (c + 1) * HG_CHUNK)
        ls = slice(p * LANES, (p + 1) * LANES)
        v_p = ops["v"][:, ls]
        o_inter = _dot_nt(ops["q_inter"][:, ls], states[p])
        states[p] = (states[p] * ops["decay"][:, ls]
                     + jnp.where(same_head, _dot_tn(v_p, ops["k_state"][:, ls]), 0.0))
        o_stack = _dot(scores, v_p)
        o_s[rs, ls] = o_inter + jnp.where(head0, o_stack[:HG_CHUNK], o_stack[HG_CHUNK:])

    pending = None
    for c in range(rows // HG_CHUNK):
        ops = chunk_operands(c)
        for p in range(pairs):
            scores = chunk_scores(ops, p)
            if pending is not None:
                chunk_output(*pending)
            pending = (c, ops, p, scores)
            interleave()
    chunk_output(*pending)
    for p in range(pairs):
        st_ref[p] = states[p]
    out_gate = gains[:, RG_WIDTH:RG_WIDTH + HG_WIDTH] * jax.nn.silu(zcol(5))
    interleave(len(side_work))
    yield

    y_b = _head_rmsnorm(o_s[...], ones_bd) * out_gate
    y = jnp.concatenate([y_a, y_b, y_c], axis=-1)
    o_ref[block_rows, :] = x_ref[block_rows, :] + _dot(y, wout_ref[...])


def _ffn_kernel(layer, last, x_ref, p_ref, ng_ref, wup_hbm, cw_ref, wdown_hbm, wgate_hbm,
                wproj_hbm, fg_ref, o_ref, ext, wup_ref, wdown_ref, wgate_ref, wproj_ref):
    rows = x_ref.shape[0]
    d_ff = wdown_ref.shape[0]

    @pl.when((pl.program_id(0) == 0) & (pl.program_id(1) == 0))
    def _():
        for hbm, dst in ((wup_hbm, wup_ref), (wdown_hbm, wdown_ref), (wgate_hbm, wgate_ref),
                         (wproj_hbm, wproj_ref)):
            n_rows, n_cols = dst.shape
            for r0 in range(0, n_rows, WEIGHT_STAGE_ROWS):
                n = min(WEIGHT_STAGE_ROWS, n_rows - r0)
                pltpu.sync_copy(hbm.at[layer, r0:r0 + n, :], ext.at[0:n, 0:n_cols])
                dst[r0:r0 + n, :] = ext[0:n, 0:n_cols].astype(BF16)

    @pl.when(pl.program_id(1) == 0)
    def _():
        ext[0:HALO, :] = jnp.zeros((HALO, ext.shape[1]), F32)

    x = x_ref[...]
    hn = _rmsnorm(x, ng_ref[layer:layer + 1, :])
    gu = _causal_conv(ext, _dot(hn, wup_ref[...]), cw_ref, FFN_CONV, rows)
    act = jax.nn.silu(gu[:, :d_ff]) * gu[:, d_ff:]
    x = x + _dot(act, wdown_ref[...])
    gate = jax.nn.sigmoid(_dot(x, wgate_ref[...]))
    x = x + gate * _dot(p_ref[...], wproj_ref[...])
    if last:
        x = _rmsnorm(x, fg_ref[...])
    o_ref[...] = x


def _const_spec(shape, grid_rank=2):
    zeros = (0,) * len(shape)
    index_map = (lambda s: zeros) if grid_rank == 1 else (lambda b, t: zeros)
    return pl.BlockSpec(shape, index_map, pipeline_mode=pl.Buffered(1))


def _layer_spec(stacked, layer, grid_rank=2):
    shape = stacked.shape[1:]
    index = (layer,) + (0,) * len(shape)
    index_map = (lambda s: index) if grid_rank == 1 else (lambda b, t: index)
    return pl.BlockSpec((None,) + shape, index_map, pipeline_mode=pl.Buffered(1))


def _block_diag(w):
    layers, n, d, e = w.shape
    eye = jnp.eye(n, dtype=w.dtype)
    return jnp.einsum('lnde,nm->lndme', w, eye).reshape(layers, n * d, n * e)


def _mixer_call(layer, x, ng, win, rgw, rgb, wri, bri, lam, lbraw, scw, og, ones_bd, wout):
    batch, seq, d_model = x.shape
    tb = min(TIME_BLOCK, seq)
    assert seq % tb == 0 and tb % HG_CHUNK == 0
    assert seq % (2 * tb) == 0
    nt = seq // (2 * tb)
    steps = batch * nt
    row_spec = pl.BlockSpec((None, 2 * tb, d_model), lambda s: (s // nt, s % nt, 0))

    def next_block(s):
        nxt = jnp.minimum(s + 1, steps - 1)
        return (nxt // nt, 2 * (nxt % nt), 0)

    next_spec = pl.BlockSpec((None, tb, d_model), next_block)
    consts = (ng, win, rgw, rgb, wri, bri, lam, lbraw, scw, og, ones_bd, wout)
    specs = [_layer_spec(c, layer, 1) if c.ndim == 3 else _const_spec(c.shape, 1) for c in consts]
    return pl.pallas_call(
        functools.partial(_mixer_kernel, layer, nt),
        grid=(steps,),
        in_specs=[row_spec, next_spec] + specs,
        out_specs=row_spec,
        out_shape=jax.ShapeDtypeStruct(x.shape, x.dtype),
        scratch_shapes=[
            pltpu.VMEM(win.shape[1:], BF16),
            pltpu.VMEM(wri.shape[1:], BF16),
            pltpu.VMEM(wout.shape[1:], BF16),
            pltpu.VMEM((tb, win.shape[2]), F32),
            pltpu.VMEM((tb, win.shape[2]), F32),
            pltpu.VMEM((tb + HALO, RG_WIDTH), F32),
            pltpu.VMEM((tb + HALO, SC_WIDTH), F32),
            pltpu.VMEM((HALO, RG_WIDTH), F32),
            pltpu.VMEM((HG_WIDTH // LANES, LANES, LANES), F32),
            pltpu.VMEM((tb, HG_WIDTH), F32),
            pltpu.VMEM((tb, HG_WIDTH), F32),
        ],
        compiler_params=pltpu.CompilerParams(
            dimension_semantics=("arbitrary",), vmem_limit_bytes=VMEM_LIMIT_BYTES),
        name=f"mixer_l{layer}",
    )(x, x, *consts)


def _ffn_call(layer, last, x, p, ng, wup, cw, wdown, wgate, wproj, fg):
    batch, seq, d_model = x.shape
    tb = min(FFN_TIME_BLOCK, seq)
    assert seq % tb == 0
    row_spec = pl.BlockSpec((None, tb, d_model), lambda b, t: (b, t, 0))
    p_spec = pl.BlockSpec((None, None, tb, p.shape[-1]), lambda b, t: (layer, b, t, 0))
    consts = (ng, wup, cw, wdown, wgate, wproj, fg)
    weights = (wup, wdown, wgate, wproj)
    assert tb >= WEIGHT_STAGE_ROWS and all(w.shape[2] <= wup.shape[2] for w in weights)
    specs = [pl.BlockSpec(memory_space=pl.ANY) if any(c is w for w in weights)
             else _layer_spec(c, layer) if c.ndim == 3 else _const_spec(c.shape) for c in consts]
    return pl.pallas_call(
        functools.partial(_ffn_kernel, layer, last),
        grid=(batch, seq // tb),
        in_specs=[row_spec, p_spec] + specs,
        out_specs=row_spec,
        out_shape=jax.ShapeDtypeStruct(x.shape, x.dtype),
        scratch_shapes=[pltpu.VMEM((tb + HALO, wup.shape[2]), F32)]
        + [pltpu.VMEM(w.shape[1:], BF16) for w in weights],
        compiler_params=pltpu.CompilerParams(
            dimension_semantics=("arbitrary", "arbitrary"), vmem_limit_bytes=VMEM_LIMIT_BYTES),
        name=f"ffn_l{layer}",
    )(x, p, *consts)


def kernel(x, p, norm_mix_gain, w_in, rg_conv_w, rg_conv_b, rg_w_r, rg_b_r, rg_w_i, rg_b_i, rg_lambda, hg_lower_bounds, sc_conv_w, mix_out_gain, w_out, norm_ffn_gain, ffn_w_up, ffn_conv_w, ffn_w_down, ple_w_proj, ple_w_gate, final_norm_gain):
    depth = w_in.shape[0]
    head_id = np.arange(RG_WIDTH) // HEAD_DIM
    ones_bd = jnp.asarray(head_id[:, None] == head_id[None, :], dtype=BF16)
    wri = jnp.concatenate([_block_diag(rg_w_r), _block_diag(rg_w_i)], axis=-1)
    bri = jnp.concatenate([rg_b_r.reshape(depth, -1), rg_b_i.reshape(depth, -1)], axis=-1)
    wup, wdown, wgate, wproj = ffn_w_up, ffn_w_down, ple_w_gate, ple_w_proj
    for i in range(depth):
        x = _mixer_call(i, x, norm_mix_gain, w_in, rg_conv_w, rg_conv_b, wri, bri, rg_lambda,
                        hg_lower_bounds, sc_conv_w, mix_out_gain, ones_bd, w_out)
        x = _ffn_call(i, i == depth - 1, x, p, norm_ffn_gain, wup, ffn_conv_w, wdown, wgate,
                      wproj, final_norm_gain.reshape(1, -1))
    return x
```

```python
import functools

import jax
import jax.numpy as jnp
import numpy as np
from jax import lax
from jax.experimental import pallas as pl
from jax.experimental.pallas import tpu as pltpu

HEAD_DIM = 64
RG_WIDTH = 384
HG_WIDTH = 384
SC_WIDTH = 256
RG_CONV = 4
SC_CONV = 3
FFN_CONV = 3
RG_C = 8.0
HG_CHUNK = 64
EPS = 1e-6
LOG2_E = 1.4426950408889634
SUBLANES = 8
LANES = 128
HALO = SUBLANES
TIME_BLOCK = 256
FFN_TIME_BLOCK = 512
PROJ_TILE = 256
WEIGHT_CAST_ROWS = 128
WEIGHT_STAGE_ROWS = 256
RG_ROW_CHUNK = 16
VMEM_LIMIT_BYTES = 56 * 1024 * 1024
MATMUL_LEVEL_HALVES = (32, 16, 8, 4, 2)

_sizes = (RG_WIDTH, RG_WIDTH, HG_WIDTH, HG_WIDTH, HG_WIDTH, HG_WIDTH, SC_WIDTH, SC_WIDTH, SC_WIDTH)
PROJ_OFFSETS = tuple(sum(_sizes[:i]) for i in range(len(_sizes) + 1))

F32 = jnp.float32
BF16 = jnp.bfloat16


def _dot(a, b):
    return jnp.dot(a.astype(BF16), b.astype(BF16), preferred_element_type=F32)


def _dot_nt(a, b):
    return lax.dot_general(a.astype(BF16), b.astype(BF16), (((1,), (1,)), ((), ())),
                           preferred_element_type=F32)


def _dot_tn(a, b):
    return lax.dot_general(a.astype(BF16), b.astype(BF16), (((0,), (0,)), ((), ())),
                           preferred_element_type=F32)


def _rmsnorm(x, gain):
    ms = jnp.mean(x * x, axis=-1, keepdims=True)
    return x * lax.rsqrt(ms + EPS) * gain


def _head_rmsnorm(y, ones_bd):
    ssum = _dot(y * y, ones_bd)
    return y * lax.rsqrt(ssum * (1.0 / HEAD_DIM) + EPS)


def _causal_conv(ext_ref, new_rows, w_ref, taps, rows):
    ext_ref[HALO:HALO + rows, :] = new_rows
    out = None
    for k in range(taps):
        term = w_ref[k:k + 1, :] * ext_ref[pl.ds(HALO - (taps - 1) + k, rows), :]
        out = term if out is None else out + term
    ext_ref[0:HALO, :] = ext_ref[rows:rows + HALO, :]
    return out


def _tile_roll(x, shift):
    rows, w = x.shape
    x3 = x.reshape(rows // SUBLANES, SUBLANES, w)
    return pltpu.roll(x3, shift, axis=1).reshape(rows, w)


def _tile_pos(shape):
    return lax.broadcasted_iota(jnp.int32, shape, 0) % SUBLANES


def _linear_scan(a, b, h0):
    rows = a.shape[0]
    pos = _tile_pos(a.shape)
    s = 1
    while s < SUBLANES:
        keep = pos >= s
        b = jnp.where(keep, a * _tile_roll(b, s) + b, b)
        a = jnp.where(keep, a * _tile_roll(a, s), a)
        s *= 2
    tiles = []
    for t in range(rows // SUBLANES):
        sl = slice(t * SUBLANES, (t + 1) * SUBLANES)
        h = a[sl] * h0 + b[sl]
        h0 = h[SUBLANES - 1:SUBLANES, :]
        tiles.append(h)
    return jnp.concatenate(tiles, axis=0)


def _chunk_cumsum(x, chunk):
    rows = x.shape[0]
    pos = _tile_pos(x.shape)
    s = 1
    while s < SUBLANES:
        x = jnp.where(pos >= s, x + _tile_roll(x, s), x)
        s *= 2
    tiles = []
    last = None
    for t in range(rows // SUBLANES):
        blk = x[t * SUBLANES:(t + 1) * SUBLANES]
        if t % (chunk // SUBLANES) != 0:
            blk = blk + last
        last = blk[SUBLANES - 1:SUBLANES, :]
        tiles.append(blk)
    return jnp.concatenate(tiles, axis=0)


def _block_row_broadcast(g, block, row):
    pieces = [jnp.broadcast_to(g[b * block + row:b * block + row + 1, :], (block, g.shape[1]))
              for b in range(g.shape[0] // block)]
    return jnp.concatenate(pieces, axis=0)


def _level_weight(g, f, half):
    if half >= 4:
        return jnp.exp2(-jnp.abs(g - _block_row_broadcast(g, 2 * half, half - 1)))
    assert half == 2
    pos = lax.broadcasted_iota(jnp.int32, f.shape, 0) % (2 * half)
    nxt = _tile_roll(f, SUBLANES - 1)
    prv = _tile_roll(f, 1)
    return jnp.where(pos == 0, nxt, jnp.where(pos == 1, 1.0, jnp.where(pos == 2, f, f * prv)))


def _mixer_kernel(layer, steps_per_seq, x_ref, xn_ref, ng_ref, win_f32, rgw_ref, rgb_ref, wri_f32,
                  bri_ref, lam_ref, lbraw_ref, scw_ref, og_ref, ones_ref, wout_f32, o_ref,
                  win_ref, wri_ref, wout_ref, z_first, z_second, ext_a, ext_c, h_carry, st_ref,
                  o_first, o_second):
    step = pl.program_id(0)
    tb = xn_ref.shape[0]
    first, second = slice(0, tb), slice(tb, 2 * tb)

    @pl.when(step == 0)
    def _():
        for src, dst in ((win_f32, win_ref), (wri_f32, wri_ref), (wout_f32, wout_ref)):
            for r0 in range(0, src.shape[0], WEIGHT_CAST_ROWS):
                rs = slice(r0, min(r0 + WEIGHT_CAST_ROWS, src.shape[0]))
                dst[rs, :] = src[rs, :].astype(BF16)

    def projection_tiles(src_ref, src_rows, dst_ref):
        hn = _rmsnorm(src_ref[src_rows, :], ng_ref[layer:layer + 1, :]).astype(BF16)

        def tile(j):
            cols = slice(j * PROJ_TILE, (j + 1) * PROJ_TILE)
            dst_ref[:, cols] = jnp.dot(hn, win_ref[:, cols], preferred_element_type=F32)

        return [functools.partial(tile, j) for j in range(win_ref.shape[1] // PROJ_TILE)]

    @pl.when(step == 0)
    def _():
        for tile in projection_tiles(x_ref, first, z_first):
            tile()

    @pl.when(step % steps_per_seq == 0)
    def _():
        ext_a[0:HALO, :] = jnp.zeros((HALO, RG_WIDTH), F32)
        ext_c[0:HALO, :] = jnp.zeros((HALO, SC_WIDTH), F32)
        h_carry[...] = jnp.zeros(h_carry.shape, F32)
        st_ref[...] = jnp.zeros(st_ref.shape, F32)

    mix = functools.partial(
        _mixer_step, layer, x_ref, rgw_ref=rgw_ref, rgb_ref=rgb_ref, wri_ref=wri_ref,
        bri_ref=bri_ref, lam_ref=lam_ref, lbraw_ref=lbraw_ref, scw_ref=scw_ref, og_ref=og_ref,
        ones_ref=ones_ref, wout_ref=wout_ref, o_ref=o_ref, ext_a=ext_a, ext_c=ext_c,
        h_carry=h_carry, st_ref=st_ref)
    blk_a = mix(first, z_first, projection_tiles(x_ref, second, z_second), o_first)
    blk_b = mix(second, z_second, projection_tiles(xn_ref, slice(None), z_first), o_second)
    next(blk_a)
    next(blk_a)
    next(blk_b)
    for _ in blk_a:
        pass
    for _ in blk_b:
        pass


def _mixer_step(layer, x_ref, block_rows, z_ref, side_work, o_s, *, rgw_ref, rgb_ref, wri_ref,
                bri_ref, lam_ref, lbraw_ref, scw_ref, og_ref, ones_ref, wout_ref, o_ref, ext_a,
                ext_c, h_carry, st_ref):
    side_work = list(side_work)

    def interleave(n=1):
        for _ in range(n):
            if side_work:
                side_work.pop(0)()

    rows = z_ref.shape[0]

    def zcol(i, row_slice=slice(None)):
        return z_ref[row_slice, PROJ_OFFSETS[i]:PROJ_OFFSETS[i + 1]]

    ones_bd = ones_ref[...]
    layer_row = slice(layer, layer + 1)
    gains = og_ref[layer_row, :]

    u = _causal_conv(ext_a, zcol(0), rgw_ref, RG_CONV, rows) + rgb_ref[layer_row, :]
    ri = _dot(u, wri_ref[...]) + bri_ref[layer_row, :]
    interleave(2)
    decay_rate = -RG_C * jax.nn.softplus(-lam_ref[layer_row, :])
    decay_rate2 = decay_rate * LOG2_E
    h_last = h_carry[0:1, :]
    gated = []
    for t0 in range(0, rows, RG_ROW_CHUNK):
        sl = slice(t0, t0 + RG_ROW_CHUNK)
        r = jax.nn.sigmoid(ri[sl, :RG_WIDTH])
        ig = jax.nn.sigmoid(ri[sl, RG_WIDTH:])
        th = jnp.tanh(r * decay_rate)
        beta = jnp.sqrt(-2.0 * th / (1.0 - th))
        h = _linear_scan(jnp.exp2(r * decay_rate2), beta * ig * u[sl], h_last)
        h_last = h[RG_ROW_CHUNK - 1:RG_ROW_CHUNK, :]
        gated.append(h * jax.nn.gelu(zcol(1, sl)))
        if t0 == rows // 2:
            interleave(2)
    h_carry[...] = jnp.broadcast_to(h_last, h_carry.shape)
    y_a = _head_rmsnorm(jnp.concatenate(gated, axis=0), ones_bd) * gains[:, 0:RG_WIDTH]
    interleave(1)

    conv = _causal_conv(ext_c, zcol(7) * zcol(8), scw_ref, SC_CONV, rows)
    y_c = (_head_rmsnorm(zcol(6) * conv, ones_bd[:SC_WIDTH, :SC_WIDTH])
           * gains[:, RG_WIDTH + HG_WIDTH:])
    yield

    raw = lbraw_ref[...]
    e = jnp.exp(raw - jnp.max(raw, axis=0, keepdims=True))
    sm = e / jnp.sum(e, axis=0, keepdims=True)
    lb = sm[0:1, :] - sm[0:1, :]
    for j in range(1, layer + 1):
        lb = lb + sm[j:j + 1, :]

    def chunk_operands(c):
        rs = slice(c * HG_CHUNK, (c + 1) * HG_CHUNK)
        f = lb + (1.0 - lb) * jax.nn.sigmoid(zcol(3, rs))
        qh = jax.nn.silu(zcol(2, rs)) * (HEAD_DIM ** -0.5)
        kk = 1.0 - f
        g = _chunk_cumsum(jnp.log2(f), HG_CHUNK)
        g_last = g[HG_CHUNK - 1:HG_CHUNK, :]
        ops = dict(decay=jnp.exp2(g_last),
                   q_inter=(qh * jnp.exp2(g)).astype(BF16),
                   k_state=(kk * jnp.exp2(g_last - g)).astype(BF16),
                   v=zcol(4, rs).astype(BF16))
        ops["band_e"] = [(qh * kk).astype(BF16), (qh * f * _tile_roll(kk, 1)).astype(BF16)]
        ops["level_q"], ops["level_k"] = [], []
        for hf in MATMUL_LEVEL_HALVES:
            w = _level_weight(g, f, hf)
            ops["level_q"].append((qh * w).astype(BF16))
            ops["level_k"].append((kk * w).astype(BF16))
        return ops

    ti = lax.broadcasted_iota(jnp.int32, (2 * HG_CHUNK, HEAD_DIM), 0) % HG_CHUNK
    si = lax.broadcasted_iota(jnp.int32, (2 * HG_CHUNK, HEAD_DIM), 1)
    top = ti ^ si
    band_masks = [ti == si, (top == 1) & (ti > si)]
    level_masks = [(top >= hf) & (top < 2 * hf) & (ti > si) for hf in MATMUL_LEVEL_HALVES]
    lane = lax.broadcasted_iota(jnp.int32, (HG_CHUNK, LANES), 1)
    head0 = lane < HEAD_DIM
    sr = lax.broadcasted_iota(jnp.int32, (LANES, LANES), 0) // HEAD_DIM
    sc = lax.broadcasted_iota(jnp.int32, (LANES, LANES), 1) // HEAD_DIM
    same_head = sr == sc
    head_ones = jnp.ones((LANES, HEAD_DIM), BF16)

    def head_stack(m):
        zero = jnp.zeros_like(m)
        return jnp.concatenate([jnp.where(head0, m, zero), jnp.where(head0, zero, m)], axis=0)

    pairs = HG_WIDTH // LANES
    states = [st_ref[p] for p in range(pairs)]
    def chunk_scores(ops, p):
        ls = slice(p * LANES, (p + 1) * LANES)
        scores = jnp.zeros((2 * HG_CHUNK, HEAD_DIM), F32)
        for mask, be in zip(band_masks, ops["band_e"]):
            scores = jnp.where(mask, _dot(head_stack(be[:, ls]), head_ones), scores)
        for mask, lq, lk in zip(level_masks, ops["level_q"], ops["level_k"]):
            scores = jnp.where(mask, _dot_nt(head_stack(lq[:, ls]), lk[:, ls]), scores)
        return scores.astype(BF16)

    def chunk_output(c, ops, p, scores):
        rs = slice(c * HG_CHUNK, (c + 1) * HG_CHUNK)
        ls = slice(p * LANES, (p + 1) * LANES)
        v_p = ops["v"][:, ls]
        o_inter = _dot_nt(ops["q_inter"][:, ls], states[p])
        states[p] = (states[p] * ops["decay"][:, ls]
                     + jnp.where(same_head, _dot_tn(v_p, ops["k_state"][:, ls]), 0.0))
        o_stack = _dot(scores, v_p)
        o_s[rs, ls] = o_inter + jnp.where(head0, o_stack[:HG_CHUNK], o_stack[HG_CHUNK:])

    pending = None
    for c in range(rows // HG_CHUNK):
        ops = chunk_operands(c)
        for p in range(pairs):
            scores = chunk_scores(ops, p)
            if pending is not None:
                chunk_output(*pending)
            pending = (c, ops, p, scores)
            interleave()
    chunk_output(*pending)
    for p in range(pairs):
        st_ref[p] = states[p]
    out_gate = gains[:, RG_WIDTH:RG_WIDTH + HG_WIDTH] * jax.nn.silu(zcol(5))
    interleave(len(side_work))
    yield

    y_b = _head_rmsnorm(o_s[...], ones_bd) * out_gate
    y = jnp.concatenate([y_a, y_b, y_c], axis=-1)
    o_ref[block_rows, :] = x_ref[block_rows, :] + _dot(y, wout_ref[...])


def _ffn_kernel(layer, last, x_ref, p_ref, ng_ref, wup_hbm, cw_ref, wdown_hbm, wgate_hbm,
                wproj_hbm, fg_ref, o_ref, ext, wup_ref, wdown_ref, wgate_ref, wproj_ref, stage_sem):
    rows = x_ref.shape[0]
    d_ff = wdown_ref.shape[0]

    @pl.when((pl.program_id(0) == 0) & (pl.program_id(1) == 0))
    def _():
        slabs = []
        for hbm, dst in ((wup_hbm, wup_ref), (wdown_hbm, wdown_ref), (wgate_hbm, wgate_ref),
                         (wproj_hbm, wproj_ref)):
            n_rows, n_cols = dst.shape
            for r0 in range(0, n_rows, WEIGHT_STAGE_ROWS):
                slabs.append((hbm, dst, r0, min(WEIGHT_STAGE_ROWS, n_rows - r0), n_cols))

        def slot_rows(i, n):
            base = (i % 2) * WEIGHT_STAGE_ROWS
            return slice(base, base + n)

        def slab_copy(i):
            hbm, _, r0, n, n_cols = slabs[i]
            return pltpu.make_async_copy(hbm.at[layer, r0:r0 + n, :],
                                         ext.at[slot_rows(i, n), 0:n_cols], stage_sem.at[i % 2])

        for i in range(min(2, len(slabs))):
            slab_copy(i).start()
        for i, (_, dst, r0, n, n_cols) in enumerate(slabs):
            slab_copy(i).wait()
            dst[r0:r0 + n, :] = ext[slot_rows(i, n), 0:n_cols].astype(BF16)
            if i + 2 < len(slabs):
                slab_copy(i + 2).start()

    @pl.when(pl.program_id(1) == 0)
    def _():
        ext[0:HALO, :] = jnp.zeros((HALO, ext.shape[1]), F32)

    x = x_ref[...]
    hn = _rmsnorm(x, ng_ref[layer:layer + 1, :])
    gu = _causal_conv(ext, _dot(hn, wup_ref[...]), cw_ref, FFN_CONV, rows)
    act = jax.nn.silu(gu[:, :d_ff]) * gu[:, d_ff:]
    x = x + _dot(act, wdown_ref[...])
    gate = jax.nn.sigmoid(_dot(x, wgate_ref[...]))
    x = x + gate * _dot(p_ref[...], wproj_ref[...])
    if last:
        x = _rmsnorm(x, fg_ref[...])
    o_ref[...] = x


def _const_spec(shape, grid_rank=2):
    zeros = (0,) * len(shape)
    index_map = (lambda s: zeros) if grid_rank == 1 else (lambda b, t: zeros)
    return pl.BlockSpec(shape, index_map, pipeline_mode=pl.Buffered(1))


def _layer_spec(stacked, layer, grid_rank=2):
    shape = stacked.shape[1:]
    index = (layer,) + (0,) * len(shape)
    index_map = (lambda s: index) if grid_rank == 1 else (lambda b, t: index)
    return pl.BlockSpec((None,) + shape, index_map, pipeline_mode=pl.Buffered(1))


def _block_diag(w):
    layers, n, d, e = w.shape
    eye = jnp.eye(n, dtype=w.dtype)
    return jnp.einsum('lnde,nm->lndme', w, eye).reshape(layers, n * d, n * e)


def _mixer_call(layer, x, ng, win, rgw, rgb, wri, bri, lam, lbraw, scw, og, ones_bd, wout):
    batch, seq, d_model = x.shape
    tb = min(TIME_BLOCK, seq)
    assert seq % tb == 0 and tb % HG_CHUNK == 0
    assert seq % (2 * tb) == 0
    nt = seq // (2 * tb)
    steps = batch * nt
    row_spec = pl.BlockSpec((None, 2 * tb, d_model), lambda s: (s // nt, s % nt, 0))

    def next_block(s):
        nxt = jnp.minimum(s + 1, steps - 1)
        return (nxt // nt, 2 * (nxt % nt), 0)

    next_spec = pl.BlockSpec((None, tb, d_model), next_block)
    consts = (ng, win, rgw, rgb, wri, bri, lam, lbraw, scw, og, ones_bd, wout)
    specs = [_layer_spec(c, layer, 1) if c.ndim == 3 else _const_spec(c.shape, 1) for c in consts]
    return pl.pallas_call(
        functools.partial(_mixer_kernel, layer, nt),
        grid=(steps,),
        in_specs=[row_spec, next_spec] + specs,
        out_specs=row_spec,
        out_shape=jax.ShapeDtypeStruct(x.shape, x.dtype),
        scratch_shapes=[
            pltpu.VMEM(win.shape[1:], BF16),
            pltpu.VMEM(wri.shape[1:], BF16),
            pltpu.VMEM(wout.shape[1:], BF16),
            pltpu.VMEM((tb, win.shape[2]), F32),
            pltpu.VMEM((tb, win.shape[2]), F32),
            pltpu.VMEM((tb + HALO, RG_WIDTH), F32),
            pltpu.VMEM((tb + HALO, SC_WIDTH), F32),
            pltpu.VMEM((HALO, RG_WIDTH), F32),
            pltpu.VMEM((HG_WIDTH // LANES, LANES, LANES), F32),
            pltpu.VMEM((tb, HG_WIDTH), F32),
            pltpu.VMEM((tb, HG_WIDTH), F32),
        ],
        compiler_params=pltpu.CompilerParams(
            dimension_semantics=("arbitrary",), vmem_limit_bytes=VMEM_LIMIT_BYTES),
        name=f"mixer_l{layer}",
    )(x, x, *consts)


def _ffn_call(layer, last, x, p, ng, wup, cw, wdown, wgate, wproj, fg):
    batch, seq, d_model = x.shape
    tb = min(FFN_TIME_BLOCK, seq)
    assert seq % tb == 0
    row_spec = pl.BlockSpec((None, tb, d_model), lambda b, t: (b, t, 0))
    p_spec = pl.BlockSpec((None, None, tb, p.shape[-1]), lambda b, t: (layer, b, t, 0))
    consts = (ng, wup, cw, wdown, wgate, wproj, fg)
    weights = (wup, wdown, wgate, wproj)
    assert tb >= 2 * WEIGHT_STAGE_ROWS and all(w.shape[2] <= wup.shape[2] for w in weights)
    specs = [pl.BlockSpec(memory_space=pl.ANY) if any(c is w for w in weights)
             else _layer_spec(c, layer) if c.ndim == 3 else _const_spec(c.shape) for c in consts]
    return pl.pallas_call(
        functools.partial(_ffn_kernel, layer, last),
        grid=(batch, seq // tb),
        in_specs=[row_spec, p_spec] + specs,
        out_specs=row_spec,
        out_shape=jax.ShapeDtypeStruct(x.shape, x.dtype),
        scratch_shapes=[pltpu.VMEM((tb + HALO, wup.shape[2]), F32)]
        + [pltpu.VMEM(w.shape[1:], BF16) for w in weights]
        + [pltpu.SemaphoreType.DMA((2,))],
        compiler_params=pltpu.CompilerParams(
            dimension_semantics=("arbitrary", "arbitrary"), vmem_limit_bytes=VMEM_LIMIT_BYTES),
        name=f"ffn_l{layer}",
    )(x, p, *consts)


def kernel(x, p, norm_mix_gain, w_in, rg_conv_w, rg_conv_b, rg_w_r, rg_b_r, rg_w_i, rg_b_i, rg_lambda, hg_lower_bounds, sc_conv_w, mix_out_gain, w_out, norm_ffn_gain, ffn_w_up, ffn_conv_w, ffn_w_down, ple_w_proj, ple_w_gate, final_norm_gain):
    depth = w_in.shape[0]
    head_id = np.arange(RG_WIDTH) // HEAD_DIM
    ones_bd = jnp.asarray(head_id[:, None] == head_id[None, :], dtype=BF16)
    wri = jnp.concatenate([_block_diag(rg_w_r), _block_diag(rg_w_i)], axis=-1)
    bri = jnp.concatenate([rg_b_r.reshape(depth, -1), rg_b_i.reshape(depth, -1)], axis=-1)
    wup, wdown, wgate, wproj = ffn_w_up, ffn_w_down, ple_w_gate, ple_w_proj
    for i in range(depth):
        x = _mixer_call(i, x, norm_mix_gain, w_in, rg_conv_w, rg_conv_b, wri, bri, rg_lambda,
                        hg_lower_bounds, sc_conv_w, mix_out_gain, ones_bd, w_out)
        x = _ffn_call(i, i == depth - 1, x, p, norm_ffn_gain, wup, ffn_conv_w, wdown, wgate,
                      wproj, final_norm_gain.reshape(1, -1))
    return x
```

```python
import functools

import jax
import jax.numpy as jnp
import numpy as np
from jax import lax
from jax.experimental import pallas as pl
from jax.experimental.pallas import tpu as pltpu

HEAD_DIM = 64
RG_WIDTH = 384
HG_WIDTH = 384
SC_WIDTH = 256
RG_CONV = 4
SC_CONV = 3
FFN_CONV = 3
RG_C = 8.0
HG_CHUNK = 64
EPS = 1e-6
LOG2_E = 1.4426950408889634
SUBLANES = 8
LANES = 128
HALO = SUBLANES
TIME_BLOCK = 256
FFN_TIME_BLOCK = 512
PROJ_TILE = 256
WEIGHT_CAST_ROWS = 128
WEIGHT_STAGE_ROWS = 256
RG_ROW_CHUNK = 16
VMEM_LIMIT_BYTES = 56 * 1024 * 1024
MATMUL_LEVEL_HALVES = (32, 16, 8, 4, 2)

_sizes = (RG_WIDTH, RG_WIDTH, HG_WIDTH, HG_WIDTH, HG_WIDTH, HG_WIDTH, SC_WIDTH, SC_WIDTH, SC_WIDTH)
PROJ_OFFSETS = tuple(sum(_sizes[:i]) for i in range(len(_sizes) + 1))

F32 = jnp.float32
BF16 = jnp.bfloat16


def _dot(a, b):
    return jnp.dot(a.astype(BF16), b.astype(BF16), preferred_element_type=F32)


def _dot_nt(a, b):
    return lax.dot_general(a.astype(BF16), b.astype(BF16), (((1,), (1,)), ((), ())),
                           preferred_element_type=F32)


def _dot_tn(a, b):
    return lax.dot_general(a.astype(BF16), b.astype(BF16), (((0,), (0,)), ((), ())),
                           preferred_element_type=F32)


def _rmsnorm(x, gain):
    ms = jnp.mean(x * x, axis=-1, keepdims=True)
    return x * lax.rsqrt(ms + EPS) * gain


def _head_rmsnorm(y, ones_bd):
    ssum = _dot(y * y, ones_bd)
    return y * lax.rsqrt(ssum * (1.0 / HEAD_DIM) + EPS)


def _causal_conv(ext_ref, new_rows, w_ref, taps, rows):
    ext_ref[HALO:HALO + rows, :] = new_rows
    out = None
    for k in range(taps):
        term = w_ref[k:k + 1, :] * ext_ref[pl.ds(HALO - (taps - 1) + k, rows), :]
        out = term if out is None else out + term
    ext_ref[0:HALO, :] = ext_ref[rows:rows + HALO, :]
    return out


def _tile_roll(x, shift):
    rows, w = x.shape
    x3 = x.reshape(rows // SUBLANES, SUBLANES, w)
    return pltpu.roll(x3, shift, axis=1).reshape(rows, w)


def _tile_pos(shape):
    return lax.broadcasted_iota(jnp.int32, shape, 0) % SUBLANES


def _linear_scan(a, b, h0):
    rows = a.shape[0]
    pos = _tile_pos(a.shape)
    s = 1
    while s < SUBLANES:
        keep = pos >= s
        b = jnp.where(keep, a * _tile_roll(b, s) + b, b)
        a = jnp.where(keep, a * _tile_roll(a, s), a)
        s *= 2
    tiles = []
    for t in range(rows // SUBLANES):
        sl = slice(t * SUBLANES, (t + 1) * SUBLANES)
        h = a[sl] * h0 + b[sl]
        h0 = h[SUBLANES - 1:SUBLANES, :]
        tiles.append(h)
    return jnp.concatenate(tiles, axis=0)


def _chunk_cumsum(x, chunk):
    rows = x.shape[0]
    pos = _tile_pos(x.shape)
    s = 1
    while s < SUBLANES:
        x = jnp.where(pos >= s, x + _tile_roll(x, s), x)
        s *= 2
    tiles = []
    last = None
    for t in range(rows // SUBLANES):
        blk = x[t * SUBLANES:(t + 1) * SUBLANES]
        if t % (chunk // SUBLANES) != 0:
            blk = blk + last
        last = blk[SUBLANES - 1:SUBLANES, :]
        tiles.append(blk)
    return jnp.concatenate(tiles, axis=0)


def _block_row_broadcast(g, block, row):
    pieces = [jnp.broadcast_to(g[b * block + row:b * block + row + 1, :], (block, g.shape[1]))
              for b in range(g.shape[0] // block)]
    return jnp.concatenate(pieces, axis=0)


def _level_weight(g, f, half):
    if half >= 4:
        return jnp.exp2(-jnp.abs(g - _block_row_broadcast(g, 2 * half, half - 1)))
    assert half == 2
    pos = lax.broadcasted_iota(jnp.int32, f.shape, 0) % (2 * half)
    nxt = _tile_roll(f, SUBLANES - 1)
    prv = _tile_roll(f, 1)
    return jnp.where(pos == 0, nxt, jnp.where(pos == 1, 1.0, jnp.where(pos == 2, f, f * prv)))


def _mixer_kernel(layer, steps_per_seq, x_ref, xn_ref, ng_ref, win_f32, rgw_ref, rgb_ref, wri_f32,
                  bri_ref, lam_ref, lbraw_ref, scw_ref, og_ref, ones_ref, wout_f32, o_ref,
                  win_ref, wri_ref, wout_ref, z_first, z_second, ext_a, ext_c, h_carry, st_ref,
                  o_first, o_second):
    step = pl.program_id(0)
    tb = xn_ref.shape[0]
    first, second = slice(0, tb), slice(tb, 2 * tb)

    @pl.when(step == 0)
    def _():
        for src, dst in ((win_f32, win_ref), (wri_f32, wri_ref), (wout_f32, wout_ref)):
            for r0 in range(0, src.shape[0], WEIGHT_CAST_ROWS):
                rs = slice(r0, min(r0 + WEIGHT_CAST_ROWS, src.shape[0]))
                dst[rs, :] = src[rs, :].astype(BF16)

    def projection_tiles(src_ref, src_rows, dst_ref):
        hn = _rmsnorm(src_ref[src_rows, :], ng_ref[layer:layer + 1, :]).astype(BF16)

        def tile(j):
            cols = slice(j * PROJ_TILE, (j + 1) * PROJ_TILE)
            dst_ref[:, cols] = jnp.dot(hn, win_ref[:, cols], preferred_element_type=F32)

        return [functools.partial(tile, j) for j in range(win_ref.shape[1] // PROJ_TILE)]

    @pl.when(step == 0)
    def _():
        for tile in projection_tiles(x_ref, first, z_first):
            tile()

    @pl.when(step % steps_per_seq == 0)
    def _():
        ext_a[0:HALO, :] = jnp.zeros((HALO, RG_WIDTH), F32)
        ext_c[0:HALO, :] = jnp.zeros((HALO, SC_WIDTH), F32)
        h_carry[...] = jnp.zeros(h_carry.shape, F32)
        st_ref[...] = jnp.zeros(st_ref.shape, F32)

    mix = functools.partial(
        _mixer_step, layer, x_ref, rgw_ref=rgw_ref, rgb_ref=rgb_ref, wri_ref=wri_ref,
        bri_ref=bri_ref, lam_ref=lam_ref, lbraw_ref=lbraw_ref, scw_ref=scw_ref, og_ref=og_ref,
        ones_ref=ones_ref, wout_ref=wout_ref, o_ref=o_ref, ext_a=ext_a, ext_c=ext_c,
        h_carry=h_carry, st_ref=st_ref)
    blk_a = mix(first, z_first, projection_tiles(x_ref, second, z_second), o_first)
    blk_b = mix(second, z_second, projection_tiles(xn_ref, slice(None), z_first), o_second)
    next(blk_a)
    next(blk_a)
    next(blk_b)
    for _ in blk_a:
        pass
    for _ in blk_b:
        pass


def _mixer_step(layer, x_ref, block_rows, z_ref, side_work, o_s, *, rgw_ref, rgb_ref, wri_ref,
                bri_ref, lam_ref, lbraw_ref, scw_ref, og_ref, ones_ref, wout_ref, o_ref, ext_a,
                ext_c, h_carry, st_ref):
    side_work = list(side_work)

    def interleave(n=1):
        for _ in range(n):
            if side_work:
                side_work.pop(0)()

    rows = z_ref.shape[0]

    def zcol(i, row_slice=slice(None)):
        return z_ref[row_slice, PROJ_OFFSETS[i]:PROJ_OFFSETS[i + 1]]

    ones_bd = ones_ref[...]
    layer_row = slice(layer, layer + 1)
    gains = og_ref[layer_row, :]

    u = _causal_conv(ext_a, zcol(0), rgw_ref, RG_CONV, rows) + rgb_ref[layer_row, :]
    ri = _dot(u, wri_ref[...]) + bri_ref[layer_row, :]
    interleave(2)
    decay_rate = -RG_C * jax.nn.softplus(-lam_ref[layer_row, :])
    decay_rate2 = decay_rate * LOG2_E
    h_last = h_carry[0:1, :]
    gated = []
    for t0 in range(0, rows, RG_ROW_CHUNK):
        sl = slice(t0, t0 + RG_ROW_CHUNK)
        r = jax.nn.sigmoid(ri[sl, :RG_WIDTH])
        ig = jax.nn.sigmoid(ri[sl, RG_WIDTH:])
        th = jnp.tanh(r * decay_rate)
        beta = jnp.sqrt(-2.0 * th / (1.0 - th))
        h = _linear_scan(jnp.exp2(r * decay_rate2), beta * ig * u[sl], h_last)
        h_last = h[RG_ROW_CHUNK - 1:RG_ROW_CHUNK, :]
        gated.append(h * jax.nn.gelu(zcol(1, sl)))
        if t0 == rows // 2:
            interleave(2)
    h_carry[...] = jnp.broadcast_to(h_last, h_carry.shape)
    y_a = _head_rmsnorm(jnp.concatenate(gated, axis=0), ones_bd) * gains[:, 0:RG_WIDTH]
    interleave(1)

    conv = _causal_conv(ext_c, zcol(7) * zcol(8), scw_ref, SC_CONV, rows)
    y_c = (_head_rmsnorm(zcol(6) * conv, ones_bd[:SC_WIDTH, :SC_WIDTH])
           * gains[:, RG_WIDTH + HG_WIDTH:])
    yield

    raw = lbraw_ref[...]
    e = jnp.exp(raw - jnp.max(raw, axis=0, keepdims=True))
    sm = e / jnp.sum(e, axis=0, keepdims=True)
    lb = sm[0:1, :] - sm[0:1, :]
    for j in range(1, layer + 1):
        lb = lb + sm[j:j + 1, :]

    def chunk_operands(c):
        rs = slice(c * HG_CHUNK, (c + 1) * HG_CHUNK)
        f = lb + (1.0 - lb) * jax.nn.sigmoid(zcol(3, rs))
        qh = jax.nn.silu(zcol(2, rs)) * (HEAD_DIM ** -0.5)
        kk = 1.0 - f
        g = _chunk_cumsum(jnp.log2(f), HG_CHUNK)
        g_last = g[HG_CHUNK - 1:HG_CHUNK, :]
        ops = dict(decay=jnp.exp2(g_last),
                   q_inter=(qh * jnp.exp2(g)).astype(BF16),
                   k_state=(kk * jnp.exp2(g_last - g)).astype(BF16),
                   v=zcol(4, rs).astype(BF16))
        ops["band_e"] = [(qh * kk).astype(BF16), (qh * f * _tile_roll(kk, 1)).astype(BF16)]
        ops["level_q"], ops["level_k"] = [], []
        for hf in MATMUL_LEVEL_HALVES:
            w = _level_weight(g, f, hf)
            ops["level_q"].append((qh * w).astype(BF16))
            ops["level_k"].append((kk * w).astype(BF16))
        return ops

    ti = lax.broadcasted_iota(jnp.int32, (2 * HG_CHUNK, HEAD_DIM), 0) % HG_CHUNK
    si = lax.broadcasted_iota(jnp.int32, (2 * HG_CHUNK, HEAD_DIM), 1)
    top = ti ^ si
    band_masks = [ti == si, (top == 1) & (ti > si)]
    level_masks = [(top >= hf) & (top < 2 * hf) & (ti > si) for hf in MATMUL_LEVEL_HALVES]
    lane = lax.broadcasted_iota(jnp.int32, (HG_CHUNK, LANES), 1)
    head0 = lane < HEAD_DIM
    sr = lax.broadcasted_iota(jnp.int32, (LANES, LANES), 0) // HEAD_DIM
    sc = lax.broadcasted_iota(jnp.int32, (LANES, LANES), 1) // HEAD_DIM
    same_head = sr == sc
    head_ones = jnp.ones((LANES, HEAD_DIM), BF16)

    def head_stack(m):
        zero = jnp.zeros_like(m)
        return jnp.concatenate([jnp.where(head0, m, zero), jnp.where(head0, zero, m)], axis=0)

    pairs = HG_WIDTH // LANES
    states = [st_ref[p] for p in range(pairs)]
    def chunk_scores(ops, p):
        ls = slice(p * LANES, (p + 1) * LANES)
        scores = jnp.zeros((2 * HG_CHUNK, HEAD_DIM), F32)
        for mask, be in zip(band_masks, ops["band_e"]):
            scores = jnp.where(mask, _dot(head_stack(be[:, ls]), head_ones), scores)
        for mask, lq, lk in zip(level_masks, ops["level_q"], ops["level_k"]):
            scores = jnp.where(mask, _dot_nt(head_stack(lq[:, ls]), lk[:, ls]), scores)
        return scores.astype(BF16)

    def chunk_output(c, ops, p, scores):
        rs = slice(c * HG_CHUNK, (c + 1) * HG_CHUNK)
        ls = slice(p * LANES, (p + 1) * LANES)
        v_p = ops["v"][:, ls]
        o_inter = _dot_nt(ops["q_inter"][:, ls], states[p])
        states[p] = (states[p] * ops["decay"][:, ls]
                     + jnp.where(same_head, _dot_tn(v_p, ops["k_state"][:, ls]), 0.0))
        o_stack = _dot(scores, v_p)
        o_s[rs, ls] = o_inter + jnp.where(head0, o_stack[:HG_CHUNK], o_stack[HG_CHUNK:])

    pending = None
    for c in range(rows // HG_CHUNK):
        ops = chunk_operands(c)
        for p in range(pairs):
            scores = chunk_scores(ops, p)
            if pending is not None:
                chunk_output(*pending)
            pending = (c, ops, p, scores)
            interleave()
    chunk_output(*pending)
    for p in range(pairs):
        st_ref[p] = states[p]
    out_gate = gains[:, RG_WIDTH:RG_WIDTH + HG_WIDTH] * jax.nn.silu(zcol(5))
    interleave(len(side_work))
    yield

    y_b = _head_rmsnorm(o_s[...], ones_bd) * out_gate
    y = jnp.concatenate([y_a, y_b, y_c], axis=-1)
    o_ref[block_rows, :] = x_ref[block_rows, :] + _dot(y, wout_ref[...])


def _ffn_kernel(layer, last, x_ref, p_ref, ng_ref, wup_hbm, cw_ref, wdown_hbm, wgate_hbm,
                wproj_hbm, fg_ref, o_ref, ext, wup_ref, wdown_ref, wgate_ref, wproj_ref, stage_sem):
    rows = x_ref.shape[0]
    d_ff = wdown_ref.shape[0]

    @pl.when((pl.program_id(0) == 0) & (pl.program_id(1) == 0))
    def _():
        slabs = []
        for hbm, dst in ((wup_hbm, wup_ref), (wdown_hbm, wdown_ref), (wgate_hbm, wgate_ref),
                         (wproj_hbm, wproj_ref)):
            n_rows, n_cols = dst.shape
            for r0 in range(0, n_rows, WEIGHT_STAGE_ROWS):
                slabs.append((hbm, dst, r0, min(WEIGHT_STAGE_ROWS, n_rows - r0), n_cols))

        def slot_rows(i, n):
            base = (i % 2) * WEIGHT_STAGE_ROWS
            return slice(base, base + n)

        def slab_copy(i):
            hbm, _, r0, n, n_cols = slabs[i]
            return pltpu.make_async_copy(hbm.at[layer, r0:r0 + n, :],
                                         ext.at[slot_rows(i, n), 0:n_cols], stage_sem.at[i % 2])

        for i in range(min(2, len(slabs))):
            slab_copy(i).start()
        for i, (_, dst, r0, n, n_cols) in enumerate(slabs):
            slab_copy(i).wait()
            dst[r0:r0 + n, :] = ext[slot_rows(i, n), 0:n_cols].astype(BF16)
            if i + 2 < len(slabs):
                slab_copy(i + 2).start()

    @pl.when(pl.program_id(1) == 0)
    def _():
        ext[0:HALO, :] = jnp.zeros((HALO, ext.shape[1]), F32)

    x = x_ref[...]
    hn = _rmsnorm(x, ng_ref[layer:layer + 1, :])
    gu = _causal_conv(ext, _dot(hn, wup_ref[...]), cw_ref, FFN_CONV, rows)
    embed = _dot(p_ref[...], wproj_ref[...])
    act = jax.nn.silu(gu[:, :d_ff]) * gu[:, d_ff:]
    x = x + _dot(act, wdown_ref[...])
    gate = jax.nn.sigmoid(_dot(x, wgate_ref[...]))
    x = x + gate * embed
    if last:
        x = _rmsnorm(x, fg_ref[...])
    o_ref[...] = x


def _const_spec(shape, grid_rank=2):
    zeros = (0,) * len(shape)
    index_map = (lambda s: zeros) if grid_rank == 1 else (lambda b, t: zeros)
    return pl.BlockSpec(shape, index_map, pipeline_mode=pl.Buffered(1))


def _layer_spec(stacked, layer, grid_rank=2):
    shape = stacked.shape[1:]
    index = (layer,) + (0,) * len(shape)
    index_map = (lambda s: index) if grid_rank == 1 else (lambda b, t: index)
    return pl.BlockSpec((None,) + shape, index_map, pipeline_mode=pl.Buffered(1))


def _block_diag(w):
    layers, n, d, e = w.shape
    eye = jnp.eye(n, dtype=w.dtype)
    return jnp.einsum('lnde,nm->lndme', w, eye).reshape(layers, n * d, n * e)


def _mixer_call(layer, x, ng, win, rgw, rgb, wri, bri, lam, lbraw, scw, og, ones_bd, wout):
    batch, seq, d_model = x.shape
    tb = min(TIME_BLOCK, seq)
    assert seq % tb == 0 and tb % HG_CHUNK == 0
    assert seq % (2 * tb) == 0
    nt = seq // (2 * tb)
    steps = batch * nt
    row_spec = pl.BlockSpec((None, 2 * tb, d_model), lambda s: (s // nt, s % nt, 0))

    def next_block(s):
        nxt = jnp.minimum(s + 1, steps - 1)
        return (nxt // nt, 2 * (nxt % nt), 0)

    next_spec = pl.BlockSpec((None, tb, d_model), next_block)
    consts = (ng, win, rgw, rgb, wri, bri, lam, lbraw, scw, og, ones_bd, wout)
    specs = [_layer_spec(c, layer, 1) if c.ndim == 3 else _const_spec(c.shape, 1) for c in consts]
    return pl.pallas_call(
        functools.partial(_mixer_kernel, layer, nt),
        grid=(steps,),
        in_specs=[row_spec, next_spec] + specs,
        out_specs=row_spec,
        out_shape=jax.ShapeDtypeStruct(x.shape, x.dtype),
        scratch_shapes=[
            pltpu.VMEM(win.shape[1:], BF16),
            pltpu.VMEM(wri.shape[1:], BF16),
            pltpu.VMEM(wout.shape[1:], BF16),
            pltpu.VMEM((tb, win.shape[2]), F32),
            pltpu.VMEM((tb, win.shape[2]), F32),
            pltpu.VMEM((tb + HALO, RG_WIDTH), F32),
            pltpu.VMEM((tb + HALO, SC_WIDTH), F32),
            pltpu.VMEM((HALO, RG_WIDTH), F32),
            pltpu.VMEM((HG_WIDTH // LANES, LANES, LANES), F32),
            pltpu.VMEM((tb, HG_WIDTH), F32),
            pltpu.VMEM((tb, HG_WIDTH), F32),
        ],
        compiler_params=pltpu.CompilerParams(
            dimension_semantics=("arbitrary",), vmem_limit_bytes=VMEM_LIMIT_BYTES),
        name=f"mixer_l{layer}",
    )(x, x, *consts)


def _ffn_call(layer, last, x, p, ng, wup, cw, wdown, wgate, wproj, fg):
    batch, seq, d_model = x.shape
    tb = min(FFN_TIME_BLOCK, seq)
    assert seq % tb == 0
    row_spec = pl.BlockSpec((None, tb, d_model), lambda b, t: (b, t, 0))
    p_spec = pl.BlockSpec((None, None, tb, p.shape[-1]), lambda b, t: (layer, b, t, 0))
    consts = (ng, wup, cw, wdown, wgate, wproj, fg)
    weights = (wup, wdown, wgate, wproj)
    assert tb >= 2 * WEIGHT_STAGE_ROWS and all(w.shape[2] <= wup.shape[2] for w in weights)
    specs = [pl.BlockSpec(memory_space=pl.ANY) if any(c is w for w in weights)
             else _layer_spec(c, layer) if c.ndim == 3 else _const_spec(c.shape) for c in consts]
    return pl.pallas_call(
        functools.partial(_ffn_kernel, layer, last),
        grid=(batch, seq // tb),
        in_specs=[row_spec, p_spec] + specs,
        out_specs=row_spec,
        out_shape=jax.ShapeDtypeStruct(x.shape, x.dtype),
        scratch_shapes=[pltpu.VMEM((tb + HALO, wup.shape[2]), F32)]
        + [pltpu.VMEM(w.shape[1:], BF16) for w in weights]
        + [pltpu.SemaphoreType.DMA((2,))],
        compiler_params=pltpu.CompilerParams(
            dimension_semantics=("arbitrary", "arbitrary"), vmem_limit_bytes=VMEM_LIMIT_BYTES),
        name=f"ffn_l{layer}",
    )(x, p, *consts)


def kernel(x, p, norm_mix_gain, w_in, rg_conv_w, rg_conv_b, rg_w_r, rg_b_r, rg_w_i, rg_b_i, rg_lambda, hg_lower_bounds, sc_conv_w, mix_out_gain, w_out, norm_ffn_gain, ffn_w_up, ffn_conv_w, ffn_w_down, ple_w_proj, ple_w_gate, final_norm_gain):
    depth = w_in.shape[0]
    head_id = np.arange(RG_WIDTH) // HEAD_DIM
    ones_bd = jnp.asarray(head_id[:, None] == head_id[None, :], dtype=BF16)
    wri = jnp.concatenate([_block_diag(rg_w_r), _block_diag(rg_w_i)], axis=-1)
    bri = jnp.concatenate([rg_b_r.reshape(depth, -1), rg_b_i.reshape(depth, -1)], axis=-1)
    wup, wdown, wgate, wproj = ffn_w_up, ffn_w_down, ple_w_gate, ple_w_proj
    for i in range(depth):
        x = _mixer_call(i, x, norm_mix_gain, w_in, rg_conv_w, rg_conv_b, wri, bri, rg_lambda,
                        hg_lower_bounds, sc_conv_w, mix_out_gain, ones_bd, w_out)
        x = _ffn_call(i, i == depth - 1, x, p, norm_ffn_gain, wup, ffn_conv_w, wdown, wgate,
                      wproj, final_norm_gain.reshape(1, -1))
    return x
```

```python
import functools

import jax
import jax.numpy as jnp
import numpy as np
from jax import lax
from jax.experimental import pallas as pl
from jax.experimental.pallas import tpu as pltpu

HEAD_DIM = 64
RG_WIDTH = 384
HG_WIDTH = 384
SC_WIDTH = 256
RG_CONV = 4
SC_CONV = 3
FFN_CONV = 3
RG_C = 8.0
HG_CHUNK = 64
EPS = 1e-6
LOG2_E = 1.4426950408889634
SUBLANES = 8
LANES = 128
HALO = SUBLANES
TIME_BLOCK = 256
FFN_TIME_BLOCK = 512
PROJ_TILE = 256
WEIGHT_CAST_ROWS = 128
WEIGHT_STAGE_ROWS = 256
RG_ROW_CHUNK = 16
VMEM_LIMIT_BYTES = 56 * 1024 * 1024
MATMUL_LEVEL_HALVES = (32, 16, 8, 4, 2)

_sizes = (RG_WIDTH, RG_WIDTH, HG_WIDTH, HG_WIDTH, HG_WIDTH, HG_WIDTH, SC_WIDTH, SC_WIDTH, SC_WIDTH)
PROJ_OFFSETS = tuple(sum(_sizes[:i]) for i in range(len(_sizes) + 1))

F32 = jnp.float32
BF16 = jnp.bfloat16


def _dot(a, b):
    return jnp.dot(a.astype(BF16), b.astype(BF16), preferred_element_type=F32)


def _dot_nt(a, b):
    return lax.dot_general(a.astype(BF16), b.astype(BF16), (((1,), (1,)), ((), ())),
                           preferred_element_type=F32)


def _dot_tn(a, b):
    return lax.dot_general(a.astype(BF16), b.astype(BF16), (((0,), (0,)), ((), ())),
                           preferred_element_type=F32)


def _rmsnorm(x, gain):
    ms = jnp.mean(x * x, axis=-1, keepdims=True)
    return x * lax.rsqrt(ms + EPS) * gain


def _head_rmsnorm(y, ones_bd):
    ssum = _dot(y * y, ones_bd)
    return y * lax.rsqrt(ssum * (1.0 / HEAD_DIM) + EPS)


def _causal_conv(ext_ref, new_rows, w_ref, taps, rows):
    ext_ref[HALO:HALO + rows, :] = new_rows
    out = None
    for k in range(taps):
        term = w_ref[k:k + 1, :] * ext_ref[pl.ds(HALO - (taps - 1) + k, rows), :]
        out = term if out is None else out + term
    ext_ref[0:HALO, :] = ext_ref[rows:rows + HALO, :]
    return out


def _tile_roll(x, shift):
    rows, w = x.shape
    x3 = x.reshape(rows // SUBLANES, SUBLANES, w)
    return pltpu.roll(x3, shift, axis=1).reshape(rows, w)


def _tile_pos(shape):
    return lax.broadcasted_iota(jnp.int32, shape, 0) % SUBLANES


def _linear_scan(a, b, h0):
    rows = a.shape[0]
    pos = _tile_pos(a.shape)
    s = 1
    while s < SUBLANES:
        keep = pos >= s
        b = jnp.where(keep, a * _tile_roll(b, s) + b, b)
        a = jnp.where(keep, a * _tile_roll(a, s), a)
        s *= 2
    tiles = []
    for t in range(rows // SUBLANES):
        sl = slice(t * SUBLANES, (t + 1) * SUBLANES)
        h = a[sl] * h0 + b[sl]
        h0 = h[SUBLANES - 1:SUBLANES, :]
        tiles.append(h)
    return jnp.concatenate(tiles, axis=0)


def _chunk_cumsum(x, chunk):
    rows = x.shape[0]
    pos = _tile_pos(x.shape)
    s = 1
    while s < SUBLANES:
        x = jnp.where(pos >= s, x + _tile_roll(x, s), x)
        s *= 2
    tiles = []
    last = None
    for t in range(rows // SUBLANES):
        blk = x[t * SUBLANES:(t + 1) * SUBLANES]
        if t % (chunk // SUBLANES) != 0:
            blk = blk + last
        last = blk[SUBLANES - 1:SUBLANES, :]
        tiles.append(blk)
    return jnp.concatenate(tiles, axis=0)


def _block_row_broadcast(g, block, row):
    pieces = [jnp.broadcast_to(g[b * block + row:b * block + row + 1, :], (block, g.shape[1]))
              for b in range(g.shape[0] // block)]
    return jnp.concatenate(pieces, axis=0)


def _level_weight(g, f, half):
    if half >= 4:
        return jnp.exp2(-jnp.abs(g - _block_row_broadcast(g, 2 * half, half - 1)))
    assert half == 2
    pos = lax.broadcasted_iota(jnp.int32, f.shape, 0) % (2 * half)
    nxt = _tile_roll(f, SUBLANES - 1)
    prv = _tile_roll(f, 1)
    return jnp.where(pos == 0, nxt, jnp.where(pos == 1, 1.0, jnp.where(pos == 2, f, f * prv)))


def _mixer_kernel(layer, steps_per_seq, x_ref, xn_ref, ng_ref, win_f32, rgw_ref, rgb_ref, wri_f32,
                  bri_ref, lam_ref, lbraw_ref, scw_ref, og_ref, ones_ref, wout_f32, o_ref,
                  win_ref, wri_ref, wout_ref, z_first, z_second, ext_a, ext_c, h_carry, st_ref,
                  o_first, o_second):
    step = pl.program_id(0)
    tb = xn_ref.shape[0]
    first, second = slice(0, tb), slice(tb, 2 * tb)

    @pl.when(step == 0)
    def _():
        for src, dst in ((win_f32, win_ref), (wri_f32, wri_ref), (wout_f32, wout_ref)):
            for r0 in range(0, src.shape[0], WEIGHT_CAST_ROWS):
                rs = slice(r0, min(r0 + WEIGHT_CAST_ROWS, src.shape[0]))
                dst[rs, :] = src[rs, :].astype(BF16)

    def projection_tiles(src_ref, src_rows, dst_ref):
        hn = _rmsnorm(src_ref[src_rows, :], ng_ref[layer:layer + 1, :]).astype(BF16)

        def tile(j):
            cols = slice(j * PROJ_TILE, (j + 1) * PROJ_TILE)
            dst_ref[:, cols] = jnp.dot(hn, win_ref[:, cols], preferred_element_type=F32)

        return [functools.partial(tile, j) for j in range(win_ref.shape[1] // PROJ_TILE)]

    @pl.when(step == 0)
    def _():
        for tile in projection_tiles(x_ref, first, z_first):
            tile()

    @pl.when(step % steps_per_seq == 0)
    def _():
        ext_a[0:HALO, :] = jnp.zeros((HALO, RG_WIDTH), F32)
        ext_c[0:HALO, :] = jnp.zeros((HALO, SC_WIDTH), F32)
        h_carry[...] = jnp.zeros(h_carry.shape, F32)
        st_ref[...] = jnp.zeros(st_ref.shape, F32)

    mix = functools.partial(
        _mixer_step, layer, x_ref, rgw_ref=rgw_ref, rgb_ref=rgb_ref, wri_ref=wri_ref,
        bri_ref=bri_ref, lam_ref=lam_ref, lbraw_ref=lbraw_ref, scw_ref=scw_ref, og_ref=og_ref,
        ones_ref=ones_ref, wout_ref=wout_ref, o_ref=o_ref, ext_a=ext_a, ext_c=ext_c,
        h_carry=h_carry, st_ref=st_ref)
    blk_a = mix(first, z_first, projection_tiles(x_ref, second, z_second), o_first)
    blk_b = mix(second, z_second, projection_tiles(xn_ref, slice(None), z_first), o_second)
    next(blk_a)
    next(blk_a)
    next(blk_b)
    for _ in blk_a:
        pass
    for _ in blk_b:
        pass


def _mixer_step(layer, x_ref, block_rows, z_ref, side_work, o_s, *, rgw_ref, rgb_ref, wri_ref,
                bri_ref, lam_ref, lbraw_ref, scw_ref, og_ref, ones_ref, wout_ref, o_ref, ext_a,
                ext_c, h_carry, st_ref):
    side_work = list(side_work)

    def interleave(n=1):
        for _ in range(n):
            if side_work:
                side_work.pop(0)()

    rows = z_ref.shape[0]

    def zcol(i, row_slice=slice(None)):
        return z_ref[row_slice, PROJ_OFFSETS[i]:PROJ_OFFSETS[i + 1]]

    ones_bd = ones_ref[...]
    layer_row = slice(layer, layer + 1)
    gains = og_ref[layer_row, :]

    u = _causal_conv(ext_a, zcol(0), rgw_ref, RG_CONV, rows) + rgb_ref[layer_row, :]
    ri = _dot(u, wri_ref[...]) + bri_ref[layer_row, :]
    interleave(2)
    decay_rate = -RG_C * jax.nn.softplus(-lam_ref[layer_row, :])
    decay_rate2 = decay_rate * LOG2_E
    h_last = h_carry[0:1, :]
    gated = []
    for t0 in range(0, rows, RG_ROW_CHUNK):
        sl = slice(t0, t0 + RG_ROW_CHUNK)
        r = jax.nn.sigmoid(ri[sl, :RG_WIDTH])
        ig = jax.nn.sigmoid(ri[sl, RG_WIDTH:])
        th = jnp.tanh(r * decay_rate)
        beta = jnp.sqrt(-2.0 * th / (1.0 - th))
        h = _linear_scan(jnp.exp2(r * decay_rate2), beta * ig * u[sl], h_last)
        h_last = h[RG_ROW_CHUNK - 1:RG_ROW_CHUNK, :]
        gated.append(h * jax.nn.gelu(zcol(1, sl)))
        if t0 == rows // 2:
            interleave(2)
    h_carry[...] = jnp.broadcast_to(h_last, h_carry.shape)
    y_a = _head_rmsnorm(jnp.concatenate(gated, axis=0), ones_bd) * gains[:, 0:RG_WIDTH]
    interleave(1)

    conv = _causal_conv(ext_c, zcol(7) * zcol(8), scw_ref, SC_CONV, rows)
    y_c = (_head_rmsnorm(zcol(6) * conv, ones_bd[:SC_WIDTH, :SC_WIDTH])
           * gains[:, RG_WIDTH + HG_WIDTH:])
    yield

    raw = lbraw_ref[...]
    e = jnp.exp(raw - jnp.max(raw, axis=0, keepdims=True))
    sm = e / jnp.sum(e, axis=0, keepdims=True)
    lb = sm[0:1, :] - sm[0:1, :]
    for j in range(1, layer + 1):
        lb = lb + sm[j:j + 1, :]

    def chunk_operands(c):
        rs = slice(c * HG_CHUNK, (c + 1) * HG_CHUNK)
        f = lb + (1.0 - lb) * jax.nn.sigmoid(zcol(3, rs))
        qh = jax.nn.silu(zcol(2, rs)) * (HEAD_DIM ** -0.5)
        kk = 1.0 - f
        g = _chunk_cumsum(jnp.log2(f), HG_CHUNK)
        g_last = g[HG_CHUNK - 1:HG_CHUNK, :]
        ops = dict(decay=jnp.exp2(g_last),
                   q_inter=(qh * jnp.exp2(g)).astype(BF16),
                   k_state=(kk * jnp.exp2(g_last - g)).astype(BF16),
                   v=zcol(4, rs).astype(BF16))
        ops["band_e"] = [(qh * kk).astype(BF16), (qh * f * _tile_roll(kk, 1)).astype(BF16)]
        ops["level_q"], ops["level_k"] = [], []
        for hf in MATMUL_LEVEL_HALVES:
            w = _level_weight(g, f, hf)
            ops["level_q"].append((qh * w).astype(BF16))
            ops["level_k"].append((kk * w).astype(BF16))
        return ops

    ti = lax.broadcasted_iota(jnp.int32, (2 * HG_CHUNK, HEAD_DIM), 0) % HG_CHUNK
    si = lax.broadcasted_iota(jnp.int32, (2 * HG_CHUNK, HEAD_DIM), 1)
    top = ti ^ si
    band_masks = [ti == si, (top == 1) & (ti > si)]
    level_masks = [(top >= hf) & (top < 2 * hf) & (ti > si) for hf in MATMUL_LEVEL_HALVES]
    lane = lax.broadcasted_iota(jnp.int32, (HG_CHUNK, LANES), 1)
    head0 = lane < HEAD_DIM
    sr = lax.broadcasted_iota(jnp.int32, (LANES, LANES), 0) // HEAD_DIM
    sc = lax.broadcasted_iota(jnp.int32, (LANES, LANES), 1) // HEAD_DIM
    same_head = sr == sc
    head_ones = jnp.ones((LANES, HEAD_DIM), BF16)

    def head_stack(m):
        zero = jnp.zeros_like(m)
        return jnp.concatenate([jnp.where(head0, m, zero), jnp.where(head0, zero, m)], axis=0)

    pairs = HG_WIDTH // LANES
    states = [st_ref[p] for p in range(pairs)]
    def chunk_scores(ops, p):
        ls = slice(p * LANES, (p + 1) * LANES)
        scores = jnp.zeros((2 * HG_CHUNK, HEAD_DIM), F32)
        for mask, be in zip(band_masks, ops["band_e"]):
            scores = jnp.where(mask, _dot(head_stack(be[:, ls]), head_ones), scores)
        for mask, lq, lk in zip(level_masks, ops["level_q"], ops["level_k"]):
            scores = jnp.where(mask, _dot_nt(head_stack(lq[:, ls]), lk[:, ls]), scores)
        return scores.astype(BF16)

    def chunk_output(c, ops, p, scores):
        rs = slice(c * HG_CHUNK, (c + 1) * HG_CHUNK)
        ls = slice(p * LANES, (p + 1) * LANES)
        v_p = ops["v"][:, ls]
        o_inter = _dot_nt(ops["q_inter"][:, ls], states[p])
        states[p] = (states[p] * ops["decay"][:, ls]
                     + jnp.where(same_head, _dot_tn(v_p, ops["k_state"][:, ls]), 0.0))
        o_stack = _dot(scores, v_p)
        o_s[rs, ls] = o_inter + jnp.where(head0, o_stack[:HG_CHUNK], o_stack[HG_CHUNK:])

    pending = None
    for c in range(rows // HG_CHUNK):
        ops = chunk_operands(c)
        for p in range(pairs):
            scores = chunk_scores(ops, p)
            if pending is not None:
                chunk_output(*pending)
            pending = (c, ops, p, scores)
            interleave()
    chunk_output(*pending)
    for p in range(pairs):
        st_ref[p] = states[p]
    out_gate = gains[:, RG_WIDTH:RG_WIDTH + HG_WIDTH] * jax.nn.silu(zcol(5))
    interleave(len(side_work))
    yield

    y_b = _head_rmsnorm(o_s[...], ones_bd) * out_gate
    y = jnp.concatenate([y_a, y_b, y_c], axis=-1)
    o_ref[block_rows, :] = x_ref[block_rows, :] + _dot(y, wout_ref[...])


def _ffn_kernel(layer, last, x_ref, p_ref, ng_ref, wup_hbm, cw_ref, wdown_hbm, wgate_hbm,
                wproj_hbm, fg_ref, o_ref, ext, wup_ref, wdown_ref, wgate_ref, wproj_ref, stage_sem):
    rows = x_ref.shape[0]
    d_ff = wdown_ref.shape[0]

    @pl.when((pl.program_id(0) == 0) & (pl.program_id(1) == 0))
    def _():
        slabs = []
        for hbm, dst in ((wup_hbm, wup_ref), (wdown_hbm, wdown_ref), (wgate_hbm, wgate_ref),
                         (wproj_hbm, wproj_ref)):
            n_rows, n_cols = dst.shape
            for r0 in range(0, n_rows, WEIGHT_STAGE_ROWS):
                slabs.append((hbm, dst, r0, min(WEIGHT_STAGE_ROWS, n_rows - r0), n_cols))

        def slot_rows(i, n):
            base = (i % 2) * WEIGHT_STAGE_ROWS
            return slice(base, base + n)

        def slab_copy(i):
            hbm, _, r0, n, n_cols = slabs[i]
            return pltpu.make_async_copy(hbm.at[layer, r0:r0 + n, :],
                                         ext.at[slot_rows(i, n), 0:n_cols], stage_sem.at[i % 2])

        for i in range(min(2, len(slabs))):
            slab_copy(i).start()
        for i, (_, dst, r0, n, n_cols) in enumerate(slabs):
            slab_copy(i).wait()
            dst[r0:r0 + n, :] = ext[slot_rows(i, n), 0:n_cols].astype(BF16)
            if i + 2 < len(slabs):
                slab_copy(i + 2).start()

    @pl.when(pl.program_id(1) == 0)
    def _():
        ext[0:HALO, :] = jnp.zeros((HALO, ext.shape[1]), F32)

    x = x_ref[...]
    gain = ng_ref[layer:layer + 1, :]
    half = rows // 2
    up = jnp.concatenate([_dot(_rmsnorm(x[:half], gain), wup_ref[...]),
                          _dot(_rmsnorm(x[half:], gain), wup_ref[...])], axis=0)
    gu = _causal_conv(ext, up, cw_ref, FFN_CONV, rows)
    embed = _dot(p_ref[...], wproj_ref[...])
    act = jax.nn.silu(gu[:, :d_ff]) * gu[:, d_ff:]
    x = x + _dot(act, wdown_ref[...])
    for part in (slice(0, half), slice(half, rows)):
        xp = x[part]
        xp = xp + jax.nn.sigmoid(_dot(xp, wgate_ref[...])) * embed[part]
        if last:
            xp = _rmsnorm(xp, fg_ref[...])
        o_ref[part, :] = xp


def _const_spec(shape, grid_rank=2):
    zeros = (0,) * len(shape)
    index_map = (lambda s: zeros) if grid_rank == 1 else (lambda b, t: zeros)
    return pl.BlockSpec(shape, index_map, pipeline_mode=pl.Buffered(1))


def _layer_spec(stacked, layer, grid_rank=2):
    shape = stacked.shape[1:]
    index = (layer,) + (0,) * len(shape)
    index_map = (lambda s: index) if grid_rank == 1 else (lambda b, t: index)
    return pl.BlockSpec((None,) + shape, index_map, pipeline_mode=pl.Buffered(1))


def _block_diag(w):
    layers, n, d, e = w.shape
    eye = jnp.eye(n, dtype=w.dtype)
    return jnp.einsum('lnde,nm->lndme', w, eye).reshape(layers, n * d, n * e)


def _mixer_call(layer, x, ng, win, rgw, rgb, wri, bri, lam, lbraw, scw, og, ones_bd, wout):
    batch, seq, d_model = x.shape
    tb = min(TIME_BLOCK, seq)
    assert seq % tb == 0 and tb % HG_CHUNK == 0
    assert seq % (2 * tb) == 0
    nt = seq // (2 * tb)
    steps = batch * nt
    row_spec = pl.BlockSpec((None, 2 * tb, d_model), lambda s: (s // nt, s % nt, 0))

    def next_block(s):
        nxt = jnp.minimum(s + 1, steps - 1)
        return (nxt // nt, 2 * (nxt % nt), 0)

    next_spec = pl.BlockSpec((None, tb, d_model), next_block)
    consts = (ng, win, rgw, rgb, wri, bri, lam, lbraw, scw, og, ones_bd, wout)
    specs = [_layer_spec(c, layer, 1) if c.ndim == 3 else _const_spec(c.shape, 1) for c in consts]
    return pl.pallas_call(
        functools.partial(_mixer_kernel, layer, nt),
        grid=(steps,),
        in_specs=[row_spec, next_spec] + specs,
        out_specs=row_spec,
        out_shape=jax.ShapeDtypeStruct(x.shape, x.dtype),
        scratch_shapes=[
            pltpu.VMEM(win.shape[1:], BF16),
            pltpu.VMEM(wri.shape[1:], BF16),
            pltpu.VMEM(wout.shape[1:], BF16),
            pltpu.VMEM((tb, win.shape[2]), F32),
            pltpu.VMEM((tb, win.shape[2]), F32),
            pltpu.VMEM((tb + HALO, RG_WIDTH), F32),
            pltpu.VMEM((tb + HALO, SC_WIDTH), F32),
            pltpu.VMEM((HALO, RG_WIDTH), F32),
            pltpu.VMEM((HG_WIDTH // LANES, LANES, LANES), F32),
            pltpu.VMEM((tb, HG_WIDTH), F32),
            pltpu.VMEM((tb, HG_WIDTH), F32),
        ],
        compiler_params=pltpu.CompilerParams(
            dimension_semantics=("arbitrary",), vmem_limit_bytes=VMEM_LIMIT_BYTES),
        name=f"mixer_l{layer}",
    )(x, x, *consts)


def _ffn_call(layer, last, x, p, ng, wup, cw, wdown, wgate, wproj, fg):
    batch, seq, d_model = x.shape
    tb = min(FFN_TIME_BLOCK, seq)
    assert seq % tb == 0
    row_spec = pl.BlockSpec((None, tb, d_model), lambda b, t: (b, t, 0))
    p_spec = pl.BlockSpec((None, None, tb, p.shape[-1]), lambda b, t: (layer, b, t, 0))
    consts = (ng, wup, cw, wdown, wgate, wproj, fg)
    weights = (wup, wdown, wgate, wproj)
    assert tb >= 2 * WEIGHT_STAGE_ROWS and all(w.shape[2] <= wup.shape[2] for w in weights)
    specs = [pl.BlockSpec(memory_space=pl.ANY) if any(c is w for w in weights)
             else _layer_spec(c, layer) if c.ndim == 3 else _const_spec(c.shape) for c in consts]
    return pl.pallas_call(
        functools.partial(_ffn_kernel, layer, last),
        grid=(batch, seq // tb),
        in_specs=[row_spec, p_spec] + specs,
        out_specs=row_spec,
        out_shape=jax.ShapeDtypeStruct(x.shape, x.dtype),
        scratch_shapes=[pltpu.VMEM((tb + HALO, wup.shape[2]), F32)]
        + [pltpu.VMEM(w.shape[1:], BF16) for w in weights]
        + [pltpu.SemaphoreType.DMA((2,))],
        compiler_params=pltpu.CompilerParams(
            dimension_semantics=("arbitrary", "arbitrary"), vmem_limit_bytes=VMEM_LIMIT_BYTES),
        name=f"ffn_l{layer}",
    )(x, p, *consts)


def kernel(x, p, norm_mix_gain, w_in, rg_conv_w, rg_conv_b, rg_w_r, rg_b_r, rg_w_i, rg_b_i, rg_lambda, hg_lower_bounds, sc_conv_w, mix_out_gain, w_out, norm_ffn_gain, ffn_w_up, ffn_conv_w, ffn_w_down, ple_w_proj, ple_w_gate, final_norm_gain):
    depth = w_in.shape[0]
    head_id = np.arange(RG_WIDTH) // HEAD_DIM
    ones_bd = jnp.asarray(head_id[:, None] == head_id[None, :], dtype=BF16)
    wri = jnp.concatenate([_block_diag(rg_w_r), _block_diag(rg_w_i)], axis=-1)
    bri = jnp.concatenate([rg_b_r.reshape(depth, -1), rg_b_i.reshape(depth, -1)], axis=-1)
    wup, wdown, wgate, wproj = ffn_w_up, ffn_w_down, ple_w_gate, ple_w_proj
    for i in range(depth):
        x = _mixer_call(i, x, norm_mix_gain, w_in, rg_conv_w, rg_conv_b, wri, bri, rg_lambda,
                        hg_lower_bounds, sc_conv_w, mix_out_gain, ones_bd, w_out)
        x = _ffn_call(i, i == depth - 1, x, p, norm_ffn_gain, wup, ffn_conv_w, wdown, wgate,
                      wproj, final_norm_gain.reshape(1, -1))
    return x
```

```python
import functools

import jax
import jax.numpy as jnp
import numpy as np
from jax import lax
from jax.experimental import pallas as pl
from jax.experimental.pallas import tpu as pltpu

HEAD_DIM = 64
RG_WIDTH = 384
HG_WIDTH = 384
SC_WIDTH = 256
RG_CONV = 4
SC_CONV = 3
FFN_CONV = 3
RG_C = 8.0
HG_CHUNK = 64
EPS = 1e-6
LOG2_E = 1.4426950408889634
SUBLANES = 8
LANES = 128
HALO = SUBLANES
TIME_BLOCK = 256
FFN_TIME_BLOCK = 512
PROJ_TILE = 256
WEIGHT_CAST_ROWS = 128
WEIGHT_STAGE_ROWS = 256
RG_ROW_CHUNK = 16
VMEM_LIMIT_BYTES = 56 * 1024 * 1024
MATMUL_LEVEL_HALVES = (32, 16, 8, 4, 2)

_sizes = (RG_WIDTH, RG_WIDTH, HG_WIDTH, HG_WIDTH, HG_WIDTH, HG_WIDTH, SC_WIDTH, SC_WIDTH, SC_WIDTH)
PROJ_OFFSETS = tuple(sum(_sizes[:i]) for i in range(len(_sizes) + 1))

F32 = jnp.float32
BF16 = jnp.bfloat16


def _dot(a, b):
    return jnp.dot(a.astype(BF16), b.astype(BF16), preferred_element_type=F32)


def _dot_nt(a, b):
    return lax.dot_general(a.astype(BF16), b.astype(BF16), (((1,), (1,)), ((), ())),
                           preferred_element_type=F32)


def _dot_tn(a, b):
    return lax.dot_general(a.astype(BF16), b.astype(BF16), (((0,), (0,)), ((), ())),
                           preferred_element_type=F32)


def _rmsnorm(x, gain):
    ms = jnp.mean(x * x, axis=-1, keepdims=True)
    return x * lax.rsqrt(ms + EPS) * gain


def _head_rmsnorm(y, ones_bd):
    ssum = _dot(y * y, ones_bd)
    return y * lax.rsqrt(ssum * (1.0 / HEAD_DIM) + EPS)


def _causal_conv(ext_ref, new_rows, w_ref, taps, rows):
    ext_ref[HALO:HALO + rows, :] = new_rows
    out = None
    for k in range(taps):
        term = w_ref[k:k + 1, :] * ext_ref[pl.ds(HALO - (taps - 1) + k, rows), :]
        out = term if out is None else out + term
    ext_ref[0:HALO, :] = ext_ref[rows:rows + HALO, :]
    return out


def _tile_roll(x, shift):
    rows, w = x.shape
    x3 = x.reshape(rows // SUBLANES, SUBLANES, w)
    return pltpu.roll(x3, shift, axis=1).reshape(rows, w)


def _tile_pos(shape):
    return lax.broadcasted_iota(jnp.int32, shape, 0) % SUBLANES


def _causal_conv_rotated(halo_ref, x, w_ref, taps):
    rows, width = x.shape
    tiles = rows // SUBLANES
    full = jnp.concatenate([halo_ref[0:HALO, :], x], axis=0).reshape(tiles + 1, SUBLANES, width)
    pos = lax.broadcasted_iota(jnp.int32, (1, SUBLANES, width), 1)
    out = w_ref[taps - 1:taps, :] * x
    for d in range(1, taps):
        rot = pltpu.roll(full, d, axis=1)
        shifted = jnp.where(pos >= d, rot[1:], rot[:tiles]).reshape(rows, width)
        out = out + w_ref[taps - 1 - d:taps - d, :] * shifted
    halo_ref[0:HALO, :] = x[rows - HALO:rows]
    return out


def _linear_scan(a, b, h0):
    rows = a.shape[0]
    pos = _tile_pos(a.shape)
    s = 1
    while s < SUBLANES:
        keep = pos >= s
        b = jnp.where(keep, a * _tile_roll(b, s) + b, b)
        a = jnp.where(keep, a * _tile_roll(a, s), a)
        s *= 2
    tiles = []
    for t in range(rows // SUBLANES):
        sl = slice(t * SUBLANES, (t + 1) * SUBLANES)
        h = a[sl] * h0 + b[sl]
        h0 = h[SUBLANES - 1:SUBLANES, :]
        tiles.append(h)
    return jnp.concatenate(tiles, axis=0)


def _chunk_cumsum(x, chunk):
    rows = x.shape[0]
    pos = _tile_pos(x.shape)
    s = 1
    while s < SUBLANES:
        x = jnp.where(pos >= s, x + _tile_roll(x, s), x)
        s *= 2
    tiles = []
    last = None
    for t in range(rows // SUBLANES):
        blk = x[t * SUBLANES:(t + 1) * SUBLANES]
        if t % (chunk // SUBLANES) != 0:
            blk = blk + last
        last = blk[SUBLANES - 1:SUBLANES, :]
        tiles.append(blk)
    return jnp.concatenate(tiles, axis=0)


def _block_row_broadcast(g, block, row):
    pieces = [jnp.broadcast_to(g[b * block + row:b * block + row + 1, :], (block, g.shape[1]))
              for b in range(g.shape[0] // block)]
    return jnp.concatenate(pieces, axis=0)


def _level_weight(g, f, half):
    if half >= 4:
        return jnp.exp2(-jnp.abs(g - _block_row_broadcast(g, 2 * half, half - 1)))
    assert half == 2
    pos = lax.broadcasted_iota(jnp.int32, f.shape, 0) % (2 * half)
    nxt = _tile_roll(f, SUBLANES - 1)
    prv = _tile_roll(f, 1)
    return jnp.where(pos == 0, nxt, jnp.where(pos == 1, 1.0, jnp.where(pos == 2, f, f * prv)))


def _mixer_kernel(layer, steps_per_seq, x_ref, xn_ref, ng_ref, win_f32, rgw_ref, rgb_ref, wri_f32,
                  bri_ref, lam_ref, lbraw_ref, scw_ref, og_ref, ones_ref, wout_f32, o_ref,
                  win_ref, wri_ref, wout_ref, z_first, z_second, ext_a, ext_c, h_carry, st_ref,
                  o_first, o_second):
    step = pl.program_id(0)
    tb = xn_ref.shape[0]
    first, second = slice(0, tb), slice(tb, 2 * tb)

    @pl.when(step == 0)
    def _():
        for src, dst in ((win_f32, win_ref), (wri_f32, wri_ref), (wout_f32, wout_ref)):
            for r0 in range(0, src.shape[0], WEIGHT_CAST_ROWS):
                rs = slice(r0, min(r0 + WEIGHT_CAST_ROWS, src.shape[0]))
                dst[rs, :] = src[rs, :].astype(BF16)

    def projection_tiles(src_ref, src_rows, dst_ref):
        hn = _rmsnorm(src_ref[src_rows, :], ng_ref[layer:layer + 1, :]).astype(BF16)

        def tile(j):
            cols = slice(j * PROJ_TILE, (j + 1) * PROJ_TILE)
            dst_ref[:, cols] = jnp.dot(hn, win_ref[:, cols], preferred_element_type=F32)

        return [functools.partial(tile, j) for j in range(win_ref.shape[1] // PROJ_TILE)]

    @pl.when(step == 0)
    def _():
        for tile in projection_tiles(x_ref, first, z_first):
            tile()

    @pl.when(step % steps_per_seq == 0)
    def _():
        ext_a[0:HALO, :] = jnp.zeros((HALO, RG_WIDTH), F32)
        ext_c[0:HALO, :] = jnp.zeros((HALO, SC_WIDTH), F32)
        h_carry[...] = jnp.zeros(h_carry.shape, F32)
        st_ref[...] = jnp.zeros(st_ref.shape, F32)

    mix = functools.partial(
        _mixer_step, layer, x_ref, rgw_ref=rgw_ref, rgb_ref=rgb_ref, wri_ref=wri_ref,
        bri_ref=bri_ref, lam_ref=lam_ref, lbraw_ref=lbraw_ref, scw_ref=scw_ref, og_ref=og_ref,
        ones_ref=ones_ref, wout_ref=wout_ref, o_ref=o_ref, ext_a=ext_a, ext_c=ext_c,
        h_carry=h_carry, st_ref=st_ref)
    blk_a = mix(first, z_first, projection_tiles(x_ref, second, z_second), o_first)
    blk_b = mix(second, z_second, projection_tiles(xn_ref, slice(None), z_first), o_second)
    next(blk_a)
    next(blk_a)
    next(blk_b)
    for _ in blk_a:
        pass
    for _ in blk_b:
        pass


def _mixer_step(layer, x_ref, block_rows, z_ref, side_work, o_s, *, rgw_ref, rgb_ref, wri_ref,
                bri_ref, lam_ref, lbraw_ref, scw_ref, og_ref, ones_ref, wout_ref, o_ref, ext_a,
                ext_c, h_carry, st_ref):
    side_work = list(side_work)

    def interleave(n=1):
        for _ in range(n):
            if side_work:
                side_work.pop(0)()

    rows = z_ref.shape[0]

    def zcol(i, row_slice=slice(None)):
        return z_ref[row_slice, PROJ_OFFSETS[i]:PROJ_OFFSETS[i + 1]]

    ones_bd = ones_ref[...]
    layer_row = slice(layer, layer + 1)
    gains = og_ref[layer_row, :]

    u = _causal_conv(ext_a, zcol(0), rgw_ref, RG_CONV, rows) + rgb_ref[layer_row, :]
    ri = _dot(u, wri_ref[...]) + bri_ref[layer_row, :]
    interleave(2)
    decay_rate = -RG_C * jax.nn.softplus(-lam_ref[layer_row, :])
    decay_rate2 = decay_rate * LOG2_E
    h_last = h_carry[0:1, :]
    gated = []
    for t0 in range(0, rows, RG_ROW_CHUNK):
        sl = slice(t0, t0 + RG_ROW_CHUNK)
        r = jax.nn.sigmoid(ri[sl, :RG_WIDTH])
        ig = jax.nn.sigmoid(ri[sl, RG_WIDTH:])
        th = jnp.tanh(r * decay_rate)
        beta = jnp.sqrt(-2.0 * th / (1.0 - th))
        h = _linear_scan(jnp.exp2(r * decay_rate2), beta * ig * u[sl], h_last)
        h_last = h[RG_ROW_CHUNK - 1:RG_ROW_CHUNK, :]
        gated.append(h * jax.nn.gelu(zcol(1, sl)))
        if t0 == rows // 2:
            interleave(2)
    h_carry[...] = jnp.broadcast_to(h_last, h_carry.shape)
    y_a = _head_rmsnorm(jnp.concatenate(gated, axis=0), ones_bd) * gains[:, 0:RG_WIDTH]
    interleave(1)

    conv = _causal_conv(ext_c, zcol(7) * zcol(8), scw_ref, SC_CONV, rows)
    y_c = (_head_rmsnorm(zcol(6) * conv, ones_bd[:SC_WIDTH, :SC_WIDTH])
           * gains[:, RG_WIDTH + HG_WIDTH:])
    yield

    raw = lbraw_ref[...]
    e = jnp.exp(raw - jnp.max(raw, axis=0, keepdims=True))
    sm = e / jnp.sum(e, axis=0, keepdims=True)
    lb = sm[0:1, :] - sm[0:1, :]
    for j in range(1, layer + 1):
        lb = lb + sm[j:j + 1, :]

    def chunk_operands(c):
        rs = slice(c * HG_CHUNK, (c + 1) * HG_CHUNK)
        f = lb + (1.0 - lb) * jax.nn.sigmoid(zcol(3, rs))
        qh = jax.nn.silu(zcol(2, rs)) * (HEAD_DIM ** -0.5)
        kk = 1.0 - f
        g = _chunk_cumsum(jnp.log2(f), HG_CHUNK)
        g_last = g[HG_CHUNK - 1:HG_CHUNK, :]
        ops = dict(decay=jnp.exp2(g_last),
                   q_inter=(qh * jnp.exp2(g)).astype(BF16),
                   k_state=(kk * jnp.exp2(g_last - g)).astype(BF16),
                   v=zcol(4, rs).astype(BF16))
        ops["band_e"] = [(qh * kk).astype(BF16), (qh * f * _tile_roll(kk, 1)).astype(BF16)]
        ops["level_q"], ops["level_k"] = [], []
        for hf in MATMUL_LEVEL_HALVES:
            w = _level_weight(g, f, hf)
            ops["level_q"].append((qh * w).astype(BF16))
            ops["level_k"].append((kk * w).astype(BF16))
        return ops

    ti = lax.broadcasted_iota(jnp.int32, (2 * HG_CHUNK, HEAD_DIM), 0) % HG_CHUNK
    si = lax.broadcasted_iota(jnp.int32, (2 * HG_CHUNK, HEAD_DIM), 1)
    top = ti ^ si
    band_masks = [ti == si, (top == 1) & (ti > si)]
    level_masks = [(top >= hf) & (top < 2 * hf) & (ti > si) for hf in MATMUL_LEVEL_HALVES]
    lane = lax.broadcasted_iota(jnp.int32, (HG_CHUNK, LANES), 1)
    head0 = lane < HEAD_DIM
    sr = lax.broadcasted_iota(jnp.int32, (LANES, LANES), 0) // HEAD_DIM
    sc = lax.broadcasted_iota(jnp.int32, (LANES, LANES), 1) // HEAD_DIM
    same_head = sr == sc
    head_ones = jnp.ones((LANES, HEAD_DIM), BF16)

    def head_stack(m):
        zero = jnp.zeros_like(m)
        return jnp.concatenate([jnp.where(head0, m, zero), jnp.where(head0, zero, m)], axis=0)

    pairs = HG_WIDTH // LANES
    states = [st_ref[p] for p in range(pairs)]
    def chunk_scores(ops, p):
        ls = slice(p * LANES, (p + 1) * LANES)
        scores = jnp.zeros((2 * HG_CHUNK, HEAD_DIM), F32)
        for mask, be in zip(band_masks, ops["band_e"]):
            scores = jnp.where(mask, _dot(head_stack(be[:, ls]), head_ones), scores)
        for mask, lq, lk in zip(level_masks, ops["level_q"], ops["level_k"]):
            scores = jnp.where(mask, _dot_nt(head_stack(lq[:, ls]), lk[:, ls]), scores)
        return scores.astype(BF16)

    def chunk_output(c, ops, p, scores):
        rs = slice(c * HG_CHUNK, (c + 1) * HG_CHUNK)
        ls = slice(p * LANES, (p + 1) * LANES)
        v_p = ops["v"][:, ls]
        o_inter = _dot_nt(ops["q_inter"][:, ls], states[p])
        states[p] = (states[p] * ops["decay"][:, ls]
                     + jnp.where(same_head, _dot_tn(v_p, ops["k_state"][:, ls]), 0.0))
        o_stack = _dot(scores, v_p)
        o_s[rs, ls] = o_inter + jnp.where(head0, o_stack[:HG_CHUNK], o_stack[HG_CHUNK:])

    pending = None
    for c in range(rows // HG_CHUNK):
        ops = chunk_operands(c)
        for p in range(pairs):
            scores = chunk_scores(ops, p)
            if pending is not None:
                chunk_output(*pending)
            pending = (c, ops, p, scores)
            interleave()
    chunk_output(*pending)
    for p in range(pairs):
        st_ref[p] = states[p]
    out_gate = gains[:, RG_WIDTH:RG_WIDTH + HG_WIDTH] * jax.nn.silu(zcol(5))
    interleave(len(side_work))
    yield

    y_b = _head_rmsnorm(o_s[...], ones_bd) * out_gate
    y = jnp.concatenate([y_a, y_b, y_c], axis=-1)
    o_ref[block_rows, :] = x_ref[block_rows, :] + _dot(y, wout_ref[...])


def _ffn_kernel(layer, last, x_ref, p_ref, ng_ref, wup_hbm, cw_ref, wdown_hbm, wgate_hbm,
                wproj_hbm, fg_ref, o_ref, ext, wup_ref, wdown_ref, wgate_ref, wproj_ref, stage_sem):
    rows = x_ref.shape[0]
    d_ff = wdown_ref.shape[0]

    @pl.when((pl.program_id(0) == 0) & (pl.program_id(1) == 0))
    def _():
        slabs = []
        for hbm, dst in ((wup_hbm, wup_ref), (wdown_hbm, wdown_ref), (wgate_hbm, wgate_ref),
                         (wproj_hbm, wproj_ref)):
            n_rows, n_cols = dst.shape
            for r0 in range(0, n_rows, WEIGHT_STAGE_ROWS):
                slabs.append((hbm, dst, r0, min(WEIGHT_STAGE_ROWS, n_rows - r0), n_cols))

        def slot_rows(i, n):
            base = (i % 2) * WEIGHT_STAGE_ROWS
            return slice(base, base + n)

        def slab_copy(i):
            hbm, _, r0, n, n_cols = slabs[i]
            return pltpu.make_async_copy(hbm.at[layer, r0:r0 + n, :],
                                         ext.at[slot_rows(i, n), 0:n_cols], stage_sem.at[i % 2])

        for i in range(min(2, len(slabs))):
            slab_copy(i).start()
        for i, (_, dst, r0, n, n_cols) in enumerate(slabs):
            slab_copy(i).wait()
            dst[r0:r0 + n, :] = ext[slot_rows(i, n), 0:n_cols].astype(BF16)
            if i + 2 < len(slabs):
                slab_copy(i + 2).start()

    @pl.when(pl.program_id(1) == 0)
    def _():
        ext[0:HALO, :] = jnp.zeros((HALO, ext.shape[1]), F32)

    x = x_ref[...]
    hn = _rmsnorm(x, ng_ref[layer:layer + 1, :])
    gu = _causal_conv_rotated(ext, _dot(hn, wup_ref[...]), cw_ref, FFN_CONV)
    embed = _dot(p_ref[...], wproj_ref[...])
    act = jax.nn.silu(gu[:, :d_ff]) * gu[:, d_ff:]
    x = x + _dot(act, wdown_ref[...])
    gate = jax.nn.sigmoid(_dot(x, wgate_ref[...]))
    x = x + gate * embed
    if last:
        x = _rmsnorm(x, fg_ref[...])
    o_ref[...] = x


def _const_spec(shape, grid_rank=2):
    zeros = (0,) * len(shape)
    index_map = (lambda s: zeros) if grid_rank == 1 else (lambda b, t: zeros)
    return pl.BlockSpec(shape, index_map, pipeline_mode=pl.Buffered(1))


def _layer_spec(stacked, layer, grid_rank=2):
    shape = stacked.shape[1:]
    index = (layer,) + (0,) * len(shape)
    index_map = (lambda s: index) if grid_rank == 1 else (lambda b, t: index)
    return pl.BlockSpec((None,) + shape, index_map, pipeline_mode=pl.Buffered(1))


def _block_diag(w):
    layers, n, d, e = w.shape
    eye = jnp.eye(n, dtype=w.dtype)
    return jnp.einsum('lnde,nm->lndme', w, eye).reshape(layers, n * d, n * e)


def _mixer_call(layer, x, ng, win, rgw, rgb, wri, bri, lam, lbraw, scw, og, ones_bd, wout):
    batch, seq, d_model = x.shape
    tb = min(TIME_BLOCK, seq)
    assert seq % tb == 0 and tb % HG_CHUNK == 0
    assert seq % (2 * tb) == 0
    nt = seq // (2 * tb)
    steps = batch * nt
    row_spec = pl.BlockSpec((None, 2 * tb, d_model), lambda s: (s // nt, s % nt, 0))

    def next_block(s):
        nxt = jnp.minimum(s + 1, steps - 1)
        return (nxt // nt, 2 * (nxt % nt), 0)

    next_spec = pl.BlockSpec((None, tb, d_model), next_block)
    consts = (ng, win, rgw, rgb, wri, bri, lam, lbraw, scw, og, ones_bd, wout)
    specs = [_layer_spec(c, layer, 1) if c.ndim == 3 else _const_spec(c.shape, 1) for c in consts]
    return pl.pallas_call(
        functools.partial(_mixer_kernel, layer, nt),
        grid=(steps,),
        in_specs=[row_spec, next_spec] + specs,
        out_specs=row_spec,
        out_shape=jax.ShapeDtypeStruct(x.shape, x.dtype),
        scratch_shapes=[
            pltpu.VMEM(win.shape[1:], BF16),
            pltpu.VMEM(wri.shape[1:], BF16),
            pltpu.VMEM(wout.shape[1:], BF16),
            pltpu.VMEM((tb, win.shape[2]), F32),
            pltpu.VMEM((tb, win.shape[2]), F32),
            pltpu.VMEM((tb + HALO, RG_WIDTH), F32),
            pltpu.VMEM((tb + HALO, SC_WIDTH), F32),
            pltpu.VMEM((HALO, RG_WIDTH), F32),
            pltpu.VMEM((HG_WIDTH // LANES, LANES, LANES), F32),
            pltpu.VMEM((tb, HG_WIDTH), F32),
            pltpu.VMEM((tb, HG_WIDTH), F32),
        ],
        compiler_params=pltpu.CompilerParams(
            dimension_semantics=("arbitrary",), vmem_limit_bytes=VMEM_LIMIT_BYTES),
        name=f"mixer_l{layer}",
    )(x, x, *consts)


def _ffn_call(layer, last, x, p, ng, wup, cw, wdown, wgate, wproj, fg):
    batch, seq, d_model = x.shape
    tb = min(FFN_TIME_BLOCK, seq)
    assert seq % tb == 0
    row_spec = pl.BlockSpec((None, tb, d_model), lambda b, t: (b, t, 0))
    p_spec = pl.BlockSpec((None, None, tb, p.shape[-1]), lambda b, t: (layer, b, t, 0))
    consts = (ng, wup, cw, wdown, wgate, wproj, fg)
    weights = (wup, wdown, wgate, wproj)
    assert tb >= 2 * WEIGHT_STAGE_ROWS and all(w.shape[2] <= wup.shape[2] for w in weights)
    specs = [pl.BlockSpec(memory_space=pl.ANY) if any(c is w for w in weights)
             else _layer_spec(c, layer) if c.ndim == 3 else _const_spec(c.shape) for c in consts]
    return pl.pallas_call(
        functools.partial(_ffn_kernel, layer, last),
        grid=(batch, seq // tb),
        in_specs=[row_spec, p_spec] + specs,
        out_specs=row_spec,
        out_shape=jax.ShapeDtypeStruct(x.shape, x.dtype),
        scratch_shapes=[pltpu.VMEM((tb + HALO, wup.shape[2]), F32)]
        + [pltpu.VMEM(w.shape[1:], BF16) for w in weights]
        + [pltpu.SemaphoreType.DMA((2,))],
        compiler_params=pltpu.CompilerParams(
            dimension_semantics=("arbitrary", "arbitrary"), vmem_limit_bytes=VMEM_LIMIT_BYTES),
        name=f"ffn_l{layer}",
    )(x, p, *consts)


def kernel(x, p, norm_mix_gain, w_in, rg_conv_w, rg_conv_b, rg_w_r, rg_b_r, rg_w_i, rg_b_i, rg_lambda, hg_lower_bounds, sc_conv_w, mix_out_gain, w_out, norm_ffn_gain, ffn_w_up, ffn_conv_w, ffn_w_down, ple_w_proj, ple_w_gate, final_norm_gain):
    depth = w_in.shape[0]
    head_id = np.arange(RG_WIDTH) // HEAD_DIM
    ones_bd = jnp.asarray(head_id[:, None] == head_id[None, :], dtype=BF16)
    wri = jnp.concatenate([_block_diag(rg_w_r), _block_diag(rg_w_i)], axis=-1)
    bri = jnp.concatenate([rg_b_r.reshape(depth, -1), rg_b_i.reshape(depth, -1)], axis=-1)
    wup, wdown, wgate, wproj = ffn_w_up, ffn_w_down, ple_w_gate, ple_w_proj
    for i in range(depth):
        x = _mixer_call(i, x, norm_mix_gain, w_in, rg_conv_w, rg_conv_b, wri, bri, rg_lambda,
                        hg_lower_bounds, sc_conv_w, mix_out_gain, ones_bd, w_out)
        x = _ffn_call(i, i == depth - 1, x, p, norm_ffn_gain, wup, ffn_conv_w, wdown, wgate,
                      wproj, final_norm_gain.reshape(1, -1))
    return x
```
